```python
import jax, jax.numpy as jnp
from jax import lax
import numpy as np

D_MODEL = 2048
BATCH = 8
SEQ = 2048
DEPTH = 1
DEC_BATCH = 32
DEC_SEQ = 1
PAST_LEN = 16384
PAGE_SIZE = 128

MIX_WIDTH = D_MODEL
GDN_DK = 128
GDN_DV = 128
GDN_HEADS = (MIX_WIDTH // 2) // GDN_DV
GDN_WIDTH = GDN_HEADS * GDN_DV
GDN_CHUNK = 64
CONV_W = 4
CONV_DIM = 2 * GDN_HEADS * GDN_DK + GDN_WIDTH
SB_HD = 128
SB_WIDTH = MIX_WIDTH - GDN_WIDTH
SB_HEADS = SB_WIDTH // SB_HD
SB_BLOCK = 128
SB_BIAS_INIT = -8.0
D_FF = 4 * D_MODEL
PROJ_DIM = CONV_DIM + GDN_WIDTH + 2 * GDN_HEADS + 3 * SB_WIDTH
DN_ALPHA = (2.0 * DEPTH) ** 0.25
DN_BETA = (8.0 * DEPTH) ** -0.25
LN_EPS = 1e-5
RMS_EPS = 1e-6

kernel_name = "hybrid_gdn_stickbreaking_decode_step"


def _layernorm(x, g, b):
    xf = x.astype(jnp.float32)
    mu = jnp.mean(xf, axis=-1, keepdims=True)
    var = jnp.mean(jnp.square(xf - mu), axis=-1, keepdims=True)
    return ((xf - mu) * lax.rsqrt(var + LN_EPS) * g + b).astype(x.dtype)


def _rmsnorm(x, g):
    xf = x.astype(jnp.float32)
    return xf * lax.rsqrt(jnp.mean(jnp.square(xf), axis=-1, keepdims=True) + RMS_EPS) * g


def _l2norm(x):
    xf = x.astype(jnp.float32)
    return xf * lax.rsqrt(jnp.sum(jnp.square(xf), axis=-1, keepdims=True) + RMS_EPS)


def _causal_conv(u, buf, w):
    s = u.shape[1]
    full = jnp.concatenate([buf.astype(u.dtype), u], axis=1)
    out = full[:, 0:s] * w[0]
    for i in range(1, CONV_W):
        out = out + full[:, i:i + s] * w[i]
    return jax.nn.silu(out), full[:, s:]


def _gated_delta(q, k, v, beta, g, s0):
    b, s, h, dk = q.shape
    dv = v.shape[-1]
    c = min(GDN_CHUNK, s)
    pad = (-s) % c
    if pad:
        padf = lambda a: jnp.pad(a, [(0, 0), (0, pad)] + [(0, 0)] * (a.ndim - 2))
        q, k, v, beta, g = padf(q), padf(k), padf(v), padf(beta), padf(g)
    n = (s + pad) // c
    ch4 = lambda a: a.reshape(b, n, c, h, a.shape[-1]).transpose(0, 3, 1, 2, 4)
    ch3 = lambda a: a.reshape(b, n, c, h).transpose(0, 3, 1, 2)
    q, k, v, beta, g = ch4(q), ch4(k), ch4(v), ch3(beta), ch3(g)
    gc = jnp.cumsum(g, axis=-1)
    kb = k * beta[..., None]
    vb = v * beta[..., None]
    tri_incl = jnp.tril(jnp.ones((c, c), dtype=bool))
    tri_strict = jnp.tril(jnp.ones((c, c), dtype=bool), -1)
    decay = jnp.exp(jnp.where(tri_incl, gc[..., :, None] - gc[..., None, :], -jnp.inf))
    lower = jnp.where(tri_strict, jnp.einsum('bhnid,bhnjd->bhnij', kb, k) * decay, 0.0)
    a_mat = lower + jnp.eye(c, dtype=lower.dtype)
    u = lax.linalg.triangular_solve(a_mat, vb, left_side=True, lower=True, unit_diagonal=True)
    kcum = lax.linalg.triangular_solve(a_mat, kb * jnp.exp(gc)[..., None], left_side=True,
                                       lower=True, unit_diagonal=True)
    qk = jnp.where(tri_incl, jnp.einsum('bhnid,bhnjd->bhnij', q, k) * decay, 0.0)

    def step(st, inp):
        qn, kn, gn, un, kcn, qkn = inp
        v_new = un - jnp.einsum('bhcd,bhde->bhce', kcn, st)
        o = (jnp.einsum('bhcd,bhde->bhce', qn * jnp.exp(gn)[..., None], st)
             + jnp.einsum('bhij,bhje->bhie', qkn, v_new))
        g_last = gn[..., -1]
        st = (st * jnp.exp(g_last)[..., None, None]
              + jnp.einsum('bhcd,bhce->bhde', kn * jnp.exp(g_last[..., None] - gn)[..., None], v_new))
        return st, o

    mv = lambda a: jnp.moveaxis(a, 2, 0)
    s_fin, o = lax.scan(step, s0, (mv(q), mv(k), mv(gc), mv(u), mv(kcum), mv(qk)))
    o = jnp.transpose(o, (1, 0, 3, 2, 4)).reshape(b, n * c, h, dv)[:, :s]
    return o, s_fin


def _stick_breaking(q, k_new, v_new, k_past, v_past, bias):
    sq = q.shape[1]
    scale = SB_HD ** -0.5
    bias = bias.astype(jnp.float32)[None, :, None, None]
    outs = []
    for b0 in range(0, sq, SB_BLOCK):
        b1 = min(b0 + SB_BLOCK, sq)
        qb = q[:, b0:b1] * scale
        zn = jnp.einsum('bqhd,bkhd->bhqk', qb, k_new[:, :b1], preferred_element_type=jnp.float32) + bias
        valid = jnp.arange(b1)[None, :] < jnp.arange(b0, b1)[:, None]
        mn = jnp.where(valid, jax.nn.log_sigmoid(-zn), 0.0)
        suf_n = lax.cumsum(mn, axis=3, reverse=True) - mn
        an = jnp.where(valid, jnp.exp(jax.nn.log_sigmoid(zn) + suf_n), 0.0)
        out = jnp.einsum('bhqk,bkhd->bqhd', an.astype(v_new.dtype), v_new[:, :b1],
                         preferred_element_type=jnp.float32)
        if k_past is not None:
            zp = jnp.einsum('bqhd,bkhd->bhqk', qb, k_past, preferred_element_type=jnp.float32) + bias
            mp = jax.nn.log_sigmoid(-zp)
            suf_p = lax.cumsum(mp, axis=3, reverse=True) - mp + jnp.sum(mn, axis=3, keepdims=True)
            ap = jnp.exp(jax.nn.log_sigmoid(zp) + suf_p)
            out = out + jnp.einsum('bhqk,bkhd->bqhd', ap.astype(v_past.dtype), v_past,
                                   preferred_element_type=jnp.float32)
        outs.append(out)
    return jnp.concatenate(outs, axis=1)


def _layer(x, k_past, v_past, s0, conv0, w_in, conv_w, a_log, dt_bias, gdn_norm_w, sb_norm_w, sb_bias,
           w_out, ln1_g, ln1_b, w_up, w_down, ln2_g, ln2_b):
    b, s, _ = x.shape
    sizes = (CONV_DIM, GDN_WIDTH, GDN_HEADS, GDN_HEADS, SB_WIDTH, SB_WIDTH)
    idx = [int(i) for i in np.cumsum(sizes)]
    proj = x @ w_in
    conv_in, z, b_logit, a_logit, sq, sk, sv = jnp.split(proj, idx, axis=-1)

    cu, conv_new = _causal_conv(conv_in, conv0, conv_w)
    gq, gk, gv = jnp.split(cu, [GDN_HEADS * GDN_DK, 2 * GDN_HEADS * GDN_DK], axis=-1)
    gq = _l2norm(gq.reshape(b, s, GDN_HEADS, GDN_DK)) * (GDN_DK ** -0.5)
    gk = _l2norm(gk.reshape(b, s, GDN_HEADS, GDN_DK))
    gv = gv.reshape(b, s, GDN_HEADS, GDN_DV).astype(jnp.float32)
    beta = jax.nn.sigmoid(b_logit.astype(jnp.float32))
    g = -jnp.exp(a_log.astype(jnp.float32)) * jax.nn.softplus(a_logit.astype(jnp.float32) + dt_bias)
    o_gdn, s_new = _gated_delta(gq, gk, gv, beta, g, s0.astype(jnp.float32))
    o_gdn = _rmsnorm(o_gdn, gdn_norm_w) * jax.nn.silu(z.reshape(b, s, GDN_HEADS, GDN_DV).astype(jnp.float32))

    sq = sq.reshape(b, s, SB_HEADS, SB_HD)
    sk = sk.reshape(b, s, SB_HEADS, SB_HD)
    sv = sv.reshape(b, s, SB_HEADS, SB_HD)
    o_sb = _rmsnorm(_stick_breaking(sq, sk, sv, k_past, v_past, sb_bias), sb_norm_w)

    mixed = jnp.concatenate([o_gdn.reshape(b, s, GDN_WIDTH), o_sb.reshape(b, s, SB_WIDTH)],
                            axis=-1).astype(x.dtype) @ w_out
    h = _layernorm(DN_ALPHA * x + mixed, ln1_g, ln1_b)
    ff = jnp.square(jax.nn.relu(h @ w_up)) @ w_down
    y = _layernorm(DN_ALPHA * h + ff, ln2_g, ln2_b)
    return y, sk, sv, s_new.astype(x.dtype), conv_new


def setup_inputs(seed: int = 0) -> dict:
    key = jax.random.key(seed)
    ks = jax.random.split(key, 24)
    f32 = jnp.float32
    n_pages = PAST_LEN // PAGE_SIZE
    used = DEC_BATCH * n_pages
    n_phys = used + max(1, used // 4)
    nrm = lambda k, shape, sc: jax.random.normal(k, shape, f32) * sc
    page_table = jax.random.permutation(ks[0], n_phys)[:used].reshape(DEC_BATCH, n_pages).astype(jnp.int32)
    return {
        "x_prompt": nrm(ks[1], (BATCH, SEQ, D_MODEL), 1.0),
        "x_sample": nrm(ks[2], (DEC_BATCH, DEC_SEQ, D_MODEL), 1.0),
        "cache_k": nrm(ks[3], (DEPTH, n_phys, PAGE_SIZE, SB_HEADS, SB_HD), 1.0),
        "cache_v": nrm(ks[4], (DEPTH, n_phys, PAGE_SIZE, SB_HEADS, SB_HD), 1.0),
        "state_gdn": nrm(ks[5], (DEPTH, DEC_BATCH, GDN_HEADS, GDN_DK, GDN_DV), GDN_DK ** -0.5),
        "state_conv": nrm(ks[6], (DEPTH, DEC_BATCH, CONV_W - 1, CONV_DIM), 1.0),
        "page_table": page_table,
        "w_in": nrm(ks[7], (DEPTH, D_MODEL, PROJ_DIM), D_MODEL ** -0.5),
        "conv_w": nrm(ks[8], (DEPTH, CONV_W, CONV_DIM), CONV_W ** -0.5),
        "a_log": jnp.log(jax.random.uniform(ks[9], (DEPTH, GDN_HEADS), f32, 1.0, 16.0)),
        "dt_bias": nrm(ks[10], (DEPTH, GDN_HEADS), 0.1),
        "gdn_norm_w": 1.0 + nrm(ks[11], (DEPTH, GDN_DV), 0.02),
        "sb_norm_w": 1.0 + nrm(ks[12], (DEPTH, SB_HEADS, SB_HD), 0.02),
        "sb_bias": SB_BIAS_INIT + nrm(ks[20], (DEPTH, SB_HEADS), 0.1),
        "w_out": nrm(ks[13], (DEPTH, MIX_WIDTH, D_MODEL), DN_BETA * MIX_WIDTH ** -0.5),
        "ln1_g": 1.0 + nrm(ks[14], (DEPTH, D_MODEL), 0.02),
        "ln1_b": nrm(ks[15], (DEPTH, D_MODEL), 0.02),
        "w_up": nrm(ks[16], (DEPTH, D_MODEL, D_FF), D_MODEL ** -0.5),
        "w_down": nrm(ks[17], (DEPTH, D_FF, D_MODEL), DN_BETA * D_FF ** -0.5),
        "ln2_g": 1.0 + nrm(ks[18], (DEPTH, D_MODEL), 0.02),
        "ln2_b": nrm(ks[19], (DEPTH, D_MODEL), 0.02),
    }


def reference(x_prompt, x_sample, cache_k, cache_v, state_gdn, state_conv, page_table, w_in, conv_w,
              a_log, dt_bias, gdn_norm_w, sb_norm_w, sb_bias, w_out, ln1_g, ln1_b, w_up, w_down,
              ln2_g, ln2_b):
    bp, sp = x_prompt.shape[0], x_prompt.shape[1]
    db = page_table.shape[0]
    yp, ys = x_prompt, x_sample
    kp_l, vp_l, gp_l, cp_l, ks_l, vs_l, gs_l, cs_l = [], [], [], [], [], [], [], []
    for l in range(DEPTH):
        s0p = jnp.zeros((bp, GDN_HEADS, GDN_DK, GDN_DV), x_prompt.dtype)
        c0p = jnp.zeros((bp, CONV_W - 1, CONV_DIM), x_prompt.dtype)
        yp, kp, vp, gp, cp = _layer(yp, None, None, s0p, c0p, w_in[l], conv_w[l], a_log[l], dt_bias[l],
                                    gdn_norm_w[l], sb_norm_w[l], sb_bias[l], w_out[l], ln1_g[l], ln1_b[l],
                                    w_up[l], w_down[l], ln2_g[l], ln2_b[l])
        k_past = cache_k[l][page_table].reshape(db, -1, SB_HEADS, SB_HD)
        v_past = cache_v[l][page_table].reshape(db, -1, SB_HEADS, SB_HD)
        ys, kn, vn, gn, cn = _layer(ys, k_past, v_past, state_gdn[l], state_conv[l], w_in[l], conv_w[l],
                                    a_log[l], dt_bias[l], gdn_norm_w[l], sb_norm_w[l], sb_bias[l], w_out[l],
                                    ln1_g[l], ln1_b[l], w_up[l], w_down[l], ln2_g[l], ln2_b[l])
        kp_l.append(kp.reshape(bp, sp // PAGE_SIZE, PAGE_SIZE, SB_HEADS, SB_HD))
        vp_l.append(vp.reshape(bp, sp // PAGE_SIZE, PAGE_SIZE, SB_HEADS, SB_HD))
        gp_l.append(gp)
        cp_l.append(cp)
        ks_l.append(kn)
        vs_l.append(vn)
        gs_l.append(gn)
        cs_l.append(cn)
    return (yp, ys, jnp.stack(kp_l), jnp.stack(vp_l), jnp.stack(gp_l), jnp.stack(cp_l),
            jnp.stack(ks_l), jnp.stack(vs_l), jnp.stack(gs_l), jnp.stack(cs_l))
```

```python
import functools

import jax
import jax.numpy as jnp
from jax import lax
from jax.experimental import pallas as pl
from jax.experimental.pallas import tpu as pltpu

F32 = jnp.float32
BF16 = jnp.bfloat16

LANES = 128
HEAD_DIM = 128
N_HEADS = 8
HEADS_WIDTH = N_HEADS * HEAD_DIM
CONV_W = 4
CONV_DIM = 3 * HEADS_WIDTH
GDN_CHUNK = 64
PAIR = 2 * GDN_CHUNK
PAGE_SIZE = 128
LN_EPS = 1e-5
RMS_EPS = 1e-6
DN_ALPHA = 2.0 ** 0.25
VMEM_LIMIT = 52 * 1024 * 1024

_HI = lax.Precision.HIGHEST


def _cparams(n_axes):
    return pltpu.CompilerParams(dimension_semantics=("arbitrary",) * n_axes,
                                vmem_limit_bytes=VMEM_LIMIT)


def _softplus(x):
    return jnp.maximum(x, 0.0) + jnp.log1p(jnp.exp(-jnp.abs(x)))


def _dot(a, b):
    return jnp.dot(a, b, preferred_element_type=F32)


def _dot_nt(a, b):
    return lax.dot_general(a, b, (((1,), (1,)), ((), ())), preferred_element_type=F32)


def _dot_tn(a, b):
    return lax.dot_general(a, b, (((0,), (0,)), ((), ())), preferred_element_type=F32)


def _dot_hi(a, b):
    return jnp.dot(a, b, preferred_element_type=F32, precision=_HI)


def _layernorm(pre, g, b):
    mu = jnp.mean(pre, axis=-1, keepdims=True)
    d = pre - mu
    var = jnp.mean(d * d, axis=-1, keepdims=True)
    return d * lax.rsqrt(var + LN_EPS) * g + b


def _mm_kernel(x_ref, w_ref, o_ref):
    o_ref[...] = _dot(x_ref[...].astype(BF16), w_ref[...]).astype(o_ref.dtype)


def _matmul(x, w, *, tm, tn, name):
    m, k = x.shape
    n = w.shape[1]
    return pl.pallas_call(
        _mm_kernel,
        grid=(n // tn, m // tm),
        in_specs=[pl.BlockSpec((tm, k), lambda j, i: (i, 0)),
                  pl.BlockSpec((k, tn), lambda j, i: (0, j))],
        out_specs=pl.BlockSpec((tm, tn), lambda j, i: (i, j)),
        out_shape=jax.ShapeDtypeStruct((m, n), F32),
        compiler_params=_cparams(2),
        name=name,
    )(x, w)


def _gdn_prompt_kernel(conv_ref, z_ref, ba_ref, cw_ref, alog_ref, dtb_ref, nw_ref,
                       o_ref, st_ref,
                       ext_ref, q_s, k_s, v_s, beta_s, g_s, *, blk):
    j = pl.program_id(1)

    @pl.when(j == 0)
    def _():
        ext_ref[0:8, :] = jnp.zeros((8, CONV_DIM), F32)
        st_ref[...] = jnp.zeros_like(st_ref)

    @pl.when(j > 0)
    def _():
        ext_ref[0:8, :] = ext_ref[blk:blk + 8, :]

    ext_ref[8:8 + blk, :] = conv_ref[...]

    for c in range(CONV_DIM // LANES):
        cs = slice(c * LANES, (c + 1) * LANES)
        acc = ext_ref[5:5 + blk, cs] * cw_ref[0:1, cs]
        acc = acc + ext_ref[6:6 + blk, cs] * cw_ref[1:2, cs]
        acc = acc + ext_ref[7:7 + blk, cs] * cw_ref[2:3, cs]
        acc = acc + ext_ref[8:8 + blk, cs] * cw_ref[3:4, cs]
        cu = acc * jax.nn.sigmoid(acc)
        if c < 2 * N_HEADS:
            ss = jnp.sum(cu * cu, axis=-1, keepdims=True)
            cu = cu * lax.rsqrt(ss + RMS_EPS)
        if c < N_HEADS:
            q_s[:, cs] = cu * (HEAD_DIM ** -0.5)
        elif c < 2 * N_HEADS:
            k_s[:, (c - N_HEADS) * LANES:(c - N_HEADS + 1) * LANES] = cu
        else:
            v_s[:, (c - 2 * N_HEADS) * LANES:(c - 2 * N_HEADS + 1) * LANES] = cu

    ba = ba_ref[...]
    beta_s[...] = jax.nn.sigmoid(ba)
    g_s[...] = -jnp.exp(alog_ref[...]) * _softplus(ba + dtb_ref[...])

    row = lax.broadcasted_iota(jnp.int32, (PAIR, PAIR), 0)
    col = lax.broadcasted_iota(jnp.int32, (PAIR, PAIR), 1)
    same = (row // GDN_CHUNK) == (col // GDN_CHUNK)
    tri_incl = same & (row >= col)
    tri_strict = same & (row > col)
    cum_l = tri_incl.astype(F32)
    eye = (row == col).astype(F32)
    nw = nw_ref[...]
    half_rows = lax.broadcasted_iota(jnp.int32, (PAIR, 1), 0) // GDN_CHUNK

    def pair_body(p, carry):
        r0 = pl.multiple_of(p * PAIR, PAIR)
        rows = pl.ds(r0, PAIR)
        g_p = g_s[rows, :]
        beta_p = beta_s[rows, :]
        gc_all = _dot_hi(cum_l, g_p)
        gct_all = jnp.transpose(gc_all)
        for h in range(N_HEADS):
            hs = slice(h * HEAD_DIM, (h + 1) * HEAD_DIM)
            qh = q_s[rows, hs]
            kh = k_s[rows, hs]
            vh = v_s[rows, hs]
            bcol = beta_p[:, h:h + 1]
            gcol = gc_all[:, N_HEADS + h:N_HEADS + h + 1]
            grow = gct_all[N_HEADS + h:N_HEADS + h + 1, :]
            decay = jnp.exp(jnp.where(tri_incl, gcol - grow, -jnp.inf))
            kb = kh * bcol
            vb = vh * bcol
            khb = kh.astype(BF16)
            lower = jnp.where(tri_strict, _dot_nt(kb.astype(BF16), khb) * decay, 0.0)
            xp = -lower
            tinv = eye + xp
            for _ in range(5):
                xp = _dot_hi(xp, xp)
                tinv = tinv + _dot_hi(tinv, xp)
            gexp = jnp.exp(gcol)
            u = _dot_hi(tinv, vb)
            kcum = _dot_hi(tinv, kb * gexp)
            qk = jnp.where(tri_incl, _dot_nt(qh.astype(BF16), khb) * decay, 0.0).astype(BF16)
            qg = (qh * gexp).astype(BF16)
            kcb = kcum.astype(BF16)
            st = st_ref[h]
            o_halves = []
            for half in range(2):
                hr = slice(half * GDN_CHUNK, (half + 1) * GDN_CHUNK)
                in_half = half_rows == half
                stb = st.astype(BF16)
                v_new = u[hr] - _dot(kcb[hr], stb)
                v_full = jnp.where(in_half, jnp.concatenate([v_new, v_new], axis=0), 0.0).astype(BF16)
                o_halves.append(_dot(qg[hr], stb) + _dot(qk[hr], v_full))
                g_last = gcol[(half + 1) * GDN_CHUNK - 1:(half + 1) * GDN_CHUNK, :]
                kdec = jnp.where(in_half, kh * jnp.exp(g_last - gcol), 0.0).astype(BF16)
                st = st * jnp.exp(g_last) + _dot_tn(kdec, v_full)
            st_ref[h] = st
            o = jnp.concatenate(o_halves, axis=0)
            o = o * lax.rsqrt(jnp.mean(o * o, axis=-1, keepdims=True) + RMS_EPS) * nw
            zh = z_ref[rows, hs]
            o_ref[rows, hs] = (o * (zh * jax.nn.sigmoid(zh))).astype(o_ref.dtype)
        return carry

    lax.fori_loop(0, blk // PAIR, pair_body, 0)


def _gdn_prompt(p1, ba, conv_w, alog_row, dtb_row, nw_row, *, batch, seq, blk):
    nblk = seq // blk
    z_col = CONV_DIM // HEADS_WIDTH
    kern = functools.partial(_gdn_prompt_kernel, blk=blk)
    return pl.pallas_call(
        kern,
        grid=(batch, nblk),
        in_specs=[
            pl.BlockSpec((blk, CONV_DIM), lambda b, j: (b * nblk + j, 0)),
            pl.BlockSpec((blk, HEADS_WIDTH), lambda b, j: (b * nblk + j, z_col)),
            pl.BlockSpec((blk, LANES), lambda b, j: (b * nblk + j, 0)),
            pl.BlockSpec((CONV_W, CONV_DIM), lambda b, j: (0, 0)),
            pl.BlockSpec((1, LANES), lambda b, j: (0, 0)),
            pl.BlockSpec((1, LANES), lambda b, j: (0, 0)),
            pl.BlockSpec((1, HEAD_DIM), lambda b, j: (0, 0)),
        ],
        out_specs=[
            pl.BlockSpec((blk, HEADS_WIDTH), lambda b, j: (b * nblk + j, 0)),
            pl.BlockSpec((None, N_HEADS, HEAD_DIM, HEAD_DIM), lambda b, j: (b, 0, 0, 0)),
        ],
        out_shape=[
            jax.ShapeDtypeStruct((batch * seq, HEADS_WIDTH), BF16),
            jax.ShapeDtypeStruct((batch, N_HEADS, HEAD_DIM, HEAD_DIM), F32),
        ],
        scratch_shapes=[
            pltpu.VMEM((blk + 8, CONV_DIM), F32),
            pltpu.VMEM((blk, HEADS_WIDTH), F32),
            pltpu.VMEM((blk, HEADS_WIDTH), F32),
            pltpu.VMEM((blk, HEADS_WIDTH), F32),
            pltpu.VMEM((blk, LANES), F32),
            pltpu.VMEM((blk, LANES), F32),
        ],
        compiler_params=_cparams(2),
        name="gdn_prompt",
    )(p1, p1, ba, conv_w, alog_row, dtb_row, nw_row)


def _gdn_sample_kernel(u_ref, z_ref, ba_ref, c0_ref, s0_ref, cw_ref, alog_ref, dtb_ref, nw_ref,
                       o_ref, st_ref, cn_ref):
    u = u_ref[0]
    c0 = c0_ref[...]
    acc = c0[0:1] * cw_ref[0:1, :]
    acc = acc + c0[1:2] * cw_ref[1:2, :]
    acc = acc + c0[2:3] * cw_ref[2:3, :]
    acc = acc + u * cw_ref[3:4, :]
    cu = acc * jax.nn.sigmoid(acc)
    cn_ref[0:1, :] = c0[1:2]
    cn_ref[1:2, :] = c0[2:3]
    cn_ref[2:3, :] = u

    ba = ba_ref[0]
    beta_all = jax.nn.sigmoid(ba)
    g_all = -jnp.exp(alog_ref[...]) * _softplus(ba + dtb_ref[...])
    z = z_ref[0]
    nw = nw_ref[...]
    first_row = lax.broadcasted_iota(jnp.int32, (8, HEAD_DIM), 0) == 0
    for h in range(N_HEADS):
        hs = slice(h * HEAD_DIM, (h + 1) * HEAD_DIM)
        q = cu[:, hs]
        k = cu[:, HEADS_WIDTH + h * HEAD_DIM:HEADS_WIDTH + (h + 1) * HEAD_DIM]
        v = cu[:, 2 * HEADS_WIDTH + h * HEAD_DIM:2 * HEADS_WIDTH + (h + 1) * HEAD_DIM]
        q = q * lax.rsqrt(jnp.sum(q * q, axis=-1, keepdims=True) + RMS_EPS) * (HEAD_DIM ** -0.5)
        k = k * lax.rsqrt(jnp.sum(k * k, axis=-1, keepdims=True) + RMS_EPS)
        beta = beta_all[:, h:h + 1]
        g = g_all[:, N_HEADS + h:N_HEADS + h + 1]
        eg = jnp.exp(g)
        st = s0_ref[h]
        stb = st.astype(BF16)
        kcum = jnp.broadcast_to(k * beta * eg, (8, HEAD_DIM)).astype(BF16)
        qg = jnp.broadcast_to(q * eg, (8, HEAD_DIM)).astype(BF16)
        v_new = v * beta - _dot(kcum, stb)[0:1]
        qk = jnp.sum(q.astype(BF16).astype(F32) * k.astype(BF16).astype(F32), axis=-1, keepdims=True)
        o = _dot(qg, stb)[0:1] + qk * v_new.astype(BF16).astype(F32)
        k8 = jnp.where(first_row, jnp.broadcast_to(k, (8, HEAD_DIM)), 0.0).astype(BF16)
        v8 = jnp.broadcast_to(v_new, (8, HEAD_DIM)).astype(BF16)
        st_ref[h] = st * eg + _dot_tn(k8, v8)
        o = o * lax.rsqrt(jnp.mean(o * o, axis=-1, keepdims=True) + RMS_EPS) * nw
        zh = z[:, hs]
        o_ref[0, :, hs] = (o * (zh * jax.nn.sigmoid(zh))).astype(o_ref.dtype)


def _gdn_sample(conv_in, z, ba, conv0, s0, conv_w, alog_row, dtb_row, nw_row):
    db = conv_in.shape[0]
    row3 = lambda a: a.reshape(db, 1, a.shape[-1])
    full = lambda shape: pl.BlockSpec(shape, lambda b: (0,) * len(shape))
    return pl.pallas_call(
        _gdn_sample_kernel,
        grid=(db,),
        in_specs=[
            pl.BlockSpec((1, 1, CONV_DIM), lambda b: (b, 0, 0)),
            pl.BlockSpec((1, 1, HEADS_WIDTH), lambda b: (b, 0, 0)),
            pl.BlockSpec((1, 1, LANES), lambda b: (b, 0, 0)),
            pl.BlockSpec((None, CONV_W - 1, CONV_DIM), lambda b: (b, 0, 0)),
            pl.BlockSpec((None, N_HEADS, HEAD_DIM, HEAD_DIM), lambda b: (b, 0, 0, 0)),
            full((CONV_W, CONV_DIM)),
            full((1, LANES)),
            full((1, LANES)),
            full((1, HEAD_DIM)),
        ],
        out_specs=[
            pl.BlockSpec((1, 1, HEADS_WIDTH), lambda b: (b, 0, 0)),
            pl.BlockSpec((None, N_HEADS, HEAD_DIM, HEAD_DIM), lambda b: (b, 0, 0, 0)),
            pl.BlockSpec((None, CONV_W - 1, CONV_DIM), lambda b: (b, 0, 0)),
        ],
        out_shape=[
            jax.ShapeDtypeStruct((db, 1, HEADS_WIDTH), F32),
            jax.ShapeDtypeStruct((db, N_HEADS, HEAD_DIM, HEAD_DIM), F32),
            jax.ShapeDtypeStruct((db, CONV_W - 1, CONV_DIM), F32),
        ],
        compiler_params=_cparams(1),
        name="gdn_sample",
    )(row3(conv_in), row3(z), row3(ba), conv0, s0, conv_w, alog_row, dtb_row, nw_row)


def _suffix_matrix():
    j = lax.broadcasted_iota(jnp.int32, (LANES, 2 * LANES), 0)
    s = lax.broadcasted_iota(jnp.int32, (LANES, 2 * LANES), 1)
    return ((j >= s) | (s >= LANES)).astype(BF16)


def _suffix_sums(m, uj):
    m_hi = m.astype(BF16)
    m_lo = (m - m_hi.astype(F32)).astype(BF16)
    cs = _dot(m_hi, uj) + _dot(m_lo, uj)
    return cs[:, :LANES], cs[:, LANES:]


def _sb_prompt_kernel(bias_ref, q_ref, k_ref, v_ref, nw_ref, o_ref, acc_ref, r_ref, *, tq):
    h = pl.program_id(1)
    i = pl.program_id(2)
    bias = bias_ref[h]
    qb = (q_ref[...] * (HEAD_DIM ** -0.5)).astype(BF16)
    uj = _suffix_matrix()
    acc_ref[...] = jnp.zeros_like(acc_ref)
    r_ref[...] = jnp.zeros_like(r_ref)
    t_pos = i * tq + lax.broadcasted_iota(jnp.int32, (tq, LANES), 0)
    lane = lax.broadcasted_iota(jnp.int32, (tq, LANES), 1)

    def block(kstart, masked):
        ks = pl.ds(kstart, LANES)
        z = _dot_nt(qb, k_ref[ks, :].astype(BF16)) + bias
        m = -_softplus(z)
        if masked:
            valid = (kstart + lane) < t_pos
            m = jnp.where(valid, m, 0.0)
        incl, tot = _suffix_sums(m, uj)
        a = jnp.exp(z + incl + r_ref[...])
        if masked:
            a = jnp.where(valid, a, 0.0)
        acc_ref[...] += _dot(a.astype(BF16), v_ref[ks, :].astype(BF16))
        r_ref[...] += tot

    for d in range(tq // LANES):
        block(pl.multiple_of(i * tq + (tq // LANES - 1 - d) * LANES, LANES), True)

    def body(n, carry):
        block(pl.multiple_of(i * tq - (n + 1) * LANES, LANES), False)
        return carry

    lax.fori_loop(0, i * (tq // LANES), body, 0)

    o = acc_ref[...]
    o = o * lax.rsqrt(jnp.mean(o * o, axis=-1, keepdims=True) + RMS_EPS) * nw_ref[pl.ds(h, 1), :]
    o_ref[...] = o.astype(o_ref.dtype)


def _sb_prompt(p1, sk, sv, sb_bias, sb_norm_w, *, batch, seq, tq):
    nq = seq // tq
    q_col0 = (CONV_DIM + HEADS_WIDTH) // HEAD_DIM
    kern = functools.partial(_sb_prompt_kernel, tq=tq)
    grid_spec = pltpu.PrefetchScalarGridSpec(
        num_scalar_prefetch=1,
        grid=(batch, N_HEADS, nq),
        in_specs=[
            pl.BlockSpec((tq, HEAD_DIM), lambda b, h, i, bias: (b * nq + i, q_col0 + h)),
            pl.BlockSpec((seq, HEAD_DIM), lambda b, h, i, bias: (b, h)),
            pl.BlockSpec((seq, HEAD_DIM), lambda b, h, i, bias: (b, h)),
            pl.BlockSpec((N_HEADS, HEAD_DIM), lambda b, h, i, bias: (0, 0)),
        ],
        out_specs=pl.BlockSpec((tq, HEAD_DIM), lambda b, h, i, bias: (b * nq + i, h)),
        scratch_shapes=[pltpu.VMEM((tq, HEAD_DIM), F32), pltpu.VMEM((tq, LANES), F32)],
    )
    return pl.pallas_call(
        kern,
        grid_spec=grid_spec,
        out_shape=jax.ShapeDtypeStruct((batch * seq, HEADS_WIDTH), BF16),
        compiler_params=_cparams(3),
        name="sb_prompt",
    )(sb_bias, p1, sk, sv, sb_norm_w)


def _sb_sample_kernel(pt_ref, q_ref, bias_ref, nw_ref, *refs, pages_per_step):
    k_refs = refs[:pages_per_step]
    v_refs = refs[pages_per_step:2 * pages_per_step]
    o_ref, acc_ref, r_ref = refs[2 * pages_per_step:]
    i = pl.program_id(1)

    @pl.when(i == 0)
    def _():
        acc_ref[...] = jnp.zeros_like(acc_ref)
        r_ref[...] = jnp.zeros_like(r_ref)

    sub = lax.broadcasted_iota(jnp.int32, (N_HEADS, HEADS_WIDTH), 0)
    lane = lax.broadcasted_iota(jnp.int32, (N_HEADS, HEADS_WIDTH), 1)
    qblk = jnp.where(lane // HEAD_DIM == sub, q_ref[0] * (HEAD_DIM ** -0.5), 0.0).astype(BF16)
    uj = _suffix_matrix()
    bias = bias_ref[...]
    for j in range(pages_per_step):
        z = _dot_nt(qblk, k_refs[j][...].astype(BF16)) + bias
        m = -_softplus(z)
        incl, tot = _suffix_sums(m, uj)
        a = jnp.exp(z + incl + r_ref[...])
        acc_ref[...] += _dot(a.astype(BF16), v_refs[j][...].astype(BF16))
        r_ref[...] += tot

    @pl.when(i == pl.num_programs(1) - 1)
    def _():
        acc = acc_ref[...]
        hrow = lax.broadcasted_iota(jnp.int32, (N_HEADS, HEAD_DIM), 0)
        o = jnp.zeros((N_HEADS, HEAD_DIM), F32)
        for h in range(N_HEADS):
            o = o + jnp.where(hrow == h, acc[:, h * HEAD_DIM:(h + 1) * HEAD_DIM], 0.0)
        o = o * lax.rsqrt(jnp.mean(o * o, axis=-1, keepdims=True) + RMS_EPS) * nw_ref[...]
        o_ref[...] = o.astype(o_ref.dtype)


def _sb_sample(q, cache_k, cache_v, page_table, bias_tile, sb_norm_w, *, pages_per_step):
    db, n_pages = page_table.shape
    n_steps = n_pages // pages_per_step
    kern = functools.partial(_sb_sample_kernel, pages_per_step=pages_per_step)

    def page_spec(j):
        return pl.BlockSpec((None, PAGE_SIZE, HEADS_WIDTH),
                            lambda b, i, pt: (pt[b, n_pages - 1 - (i * pages_per_step + j)], 0, 0))

    grid_spec = pltpu.PrefetchScalarGridSpec(
        num_scalar_prefetch=1,
        grid=(db, n_steps),
        in_specs=[
            pl.BlockSpec((1, 1, HEADS_WIDTH), lambda b, i, pt: (b, 0, 0)),
            pl.BlockSpec((N_HEADS, LANES), lambda b, i, pt: (0, 0)),
            pl.BlockSpec((N_HEADS, HEAD_DIM), lambda b, i, pt: (0, 0)),
        ] + [page_spec(j) for j in range(pages_per_step)] * 2,
        out_specs=pl.BlockSpec((None, N_HEADS, HEAD_DIM), lambda b, i, pt: (b, 0, 0)),
        scratch_shapes=[pltpu.VMEM((N_HEADS, HEADS_WIDTH), F32), pltpu.VMEM((N_HEADS, LANES), F32)],
    )
    return pl.pallas_call(
        kern,
        grid_spec=grid_spec,
        out_shape=jax.ShapeDtypeStruct((db, N_HEADS, HEAD_DIM), F32),
        compiler_params=_cparams(2),
        name="sb_sample",
    )(page_table, q.reshape(db, 1, HEADS_WIDTH), bias_tile, sb_norm_w,
      *([cache_k] * pages_per_step), *([cache_v] * pages_per_step))


def _out_ln_kernel(og_ref, os_ref, x_ref, wg_ref, ws_ref, g_ref, b_ref, o_ref):
    mixed = _dot(og_ref[...].astype(BF16), wg_ref[...]) + _dot(os_ref[...].astype(BF16), ws_ref[...])
    o_ref[...] = _layernorm(DN_ALPHA * x_ref[...] + mixed, g_ref[...], b_ref[...])


def _out_ln(og, osb, x, wg, ws, g, b, *, tm, name):
    m, d = x.shape
    return pl.pallas_call(
        _out_ln_kernel,
        grid=(m // tm,),
        in_specs=[
            pl.BlockSpec((tm, HEADS_WIDTH), lambda i: (i, 0)),
            pl.BlockSpec((tm, HEADS_WIDTH), lambda i: (i, 0)),
            pl.BlockSpec((tm, d), lambda i: (i, 0)),
            pl.BlockSpec((HEADS_WIDTH, d), lambda i: (0, 0)),
            pl.BlockSpec((HEADS_WIDTH, d), lambda i: (0, 0)),
            pl.BlockSpec((1, d), lambda i: (0, 0)),
            pl.BlockSpec((1, d), lambda i: (0, 0)),
        ],
        out_specs=pl.BlockSpec((tm, d), lambda i: (i, 0)),
        out_shape=jax.ShapeDtypeStruct((m, d), F32),
        compiler_params=_cparams(1),
        name=name,
    )(og, osb, x, wg, ws, g, b)


def _ffn_kernel(h_ref, wu_ref, wd_ref, g_ref, b_ref, o_ref, hb_ref):
    f = pl.program_id(1)

    @pl.when(f == 0)
    def _():
        hb_ref[...] = h_ref[...].astype(BF16)
        o_ref[...] = jnp.zeros_like(o_ref)

    a = jnp.maximum(_dot(hb_ref[...], wu_ref[...]), 0.0)
    o_ref[...] += _dot((a * a).astype(BF16), wd_ref[...])

    @pl.when(f == pl.num_programs(1) - 1)
    def _():
        o_ref[...] = _layernorm(DN_ALPHA * h_ref[...] + o_ref[...], g_ref[...], b_ref[...])


def _ffn(h, wu, wd, g, b, *, tm, tf, name):
    m, d = h.shape
    d_ff = wu.shape[1]
    return pl.pallas_call(
        _ffn_kernel,
        grid=(m // tm, d_ff // tf),
        in_specs=[
            pl.BlockSpec((tm, d), lambda i, f: (i, 0)),
            pl.BlockSpec((d, tf), lambda i, f: (0, f)),
            pl.BlockSpec((tf, d), lambda i, f: (f, 0)),
            pl.BlockSpec((1, d), lambda i, f: (0, 0)),
            pl.BlockSpec((1, d), lambda i, f: (0, 0)),
        ],
        out_specs=pl.BlockSpec((tm, d), lambda i, f: (i, 0)),
        out_shape=jax.ShapeDtypeStruct((m, d), F32),
        scratch_shapes=[pltpu.VMEM((tm, d), BF16)],
        compiler_params=_cparams(2),
        name=name,
    )(h, wu, wd, g, b)


def kernel(x_prompt, x_sample, cache_k, cache_v, state_gdn, state_conv, page_table, w_in, conv_w, a_log,
           dt_bias, gdn_norm_w, sb_norm_w, sb_bias, w_out, ln1_g, ln1_b, w_up, w_down, ln2_g, ln2_b):
    bp, sp, d = x_prompt.shape
    db = x_sample.shape[0]
    n_phys = cache_k.shape[1]
    assert w_in.shape[0] == 1 and x_sample.shape[1] == 1

    w = w_in[0]
    o_z = CONV_DIM
    o_b = o_z + HEADS_WIDTH
    o_a = o_b + N_HEADS
    o_q = o_a + N_HEADS
    o_k = o_q + HEADS_WIDTH
    o_v = o_k + HEADS_WIDTH
    w_main = jnp.concatenate([w[:, :o_b], w[:, o_q:o_k]], axis=1).astype(BF16)
    w_k = w[:, o_k:o_v].astype(BF16)
    w_v = w[:, o_v:].astype(BF16)
    w_ba = jnp.pad(w[:, o_b:o_q], ((0, 0), (0, LANES - 2 * N_HEADS))).astype(BF16)
    w_og = w_out[0, :HEADS_WIDTH].astype(BF16)
    w_os = w_out[0, HEADS_WIDTH:].astype(BF16)
    wu = w_up[0].astype(BF16)
    wd = w_down[0].astype(BF16)
    pad_row = lambda v: jnp.pad(v, (N_HEADS, LANES - 2 * N_HEADS)).reshape(1, LANES)
    alog_row = pad_row(a_log[0])
    dtb_row = pad_row(dt_bias[0])
    nw_row = gdn_norm_w[0].reshape(1, HEAD_DIM)
    cw = conv_w[0]
    row = lambda v: v.reshape(1, d)
    bias_tile = jnp.broadcast_to(sb_bias[0][:, None], (N_HEADS, LANES))

    xp = x_prompt.reshape(bp * sp, d)
    p1 = _matmul(xp, w_main, tm=512, tn=1024, name="proj_main_p")
    skp = _matmul(xp, w_k, tm=512, tn=1024, name="proj_k_p")
    svp = _matmul(xp, w_v, tm=512, tn=1024, name="proj_v_p")
    bap = _matmul(xp, w_ba, tm=512, tn=LANES, name="proj_ba_p")
    og_p, gdn_p = _gdn_prompt(p1, bap, cw, alog_row, dtb_row, nw_row, batch=bp, seq=sp, blk=256)
    os_p = _sb_prompt(p1, skp, svp, sb_bias[0], sb_norm_w[0], batch=bp, seq=sp, tq=256)
    h_p = _out_ln(og_p, os_p, xp, w_og, w_os, row(ln1_g[0]), row(ln1_b[0]), tm=512, name="out_ln_p")
    y_p = _ffn(h_p, wu, wd, row(ln2_g[0]), row(ln2_b[0]), tm=512, tf=512, name="ffn_p")
    conv_p = p1.reshape(bp, sp, -1)[:, sp - (CONV_W - 1):, :CONV_DIM]

    xs = x_sample.reshape(db, d)
    s1 = _matmul(xs, w_main, tm=db, tn=1024, name="proj_main_s")
    sks = _matmul(xs, w_k, tm=db, tn=1024, name="proj_k_s")
    svs = _matmul(xs, w_v, tm=db, tn=1024, name="proj_v_s")
    bas = _matmul(xs, w_ba, tm=db, tn=LANES, name="proj_ba_s")
    og_s, gdn_s, conv_s = _gdn_sample(s1[:, :CONV_DIM], s1[:, CONV_DIM:CONV_DIM + HEADS_WIDTH], bas,
                                      state_conv[0], state_gdn[0], cw, alog_row, dtb_row, nw_row)
    os_s = _sb_sample(s1[:, CONV_DIM + HEADS_WIDTH:],
                      cache_k.reshape(n_phys, PAGE_SIZE, HEADS_WIDTH),
                      cache_v.reshape(n_phys, PAGE_SIZE, HEADS_WIDTH),
                      page_table, bias_tile, sb_norm_w[0], pages_per_step=4)
    h_s = _out_ln(og_s.reshape(db, HEADS_WIDTH), os_s.reshape(db, HEADS_WIDTH), xs, w_og, w_os,
                  row(ln1_g[0]), row(ln1_b[0]), tm=db, name="out_ln_s")
    y_s = _ffn(h_s, wu, wd, row(ln2_g[0]), row(ln2_b[0]), tm=db, tf=512, name="ffn_s")

    n_pg = sp // PAGE_SIZE
    return (
        y_p.reshape(bp, sp, d),
        y_s.reshape(db, 1, d),
        skp.reshape(1, bp, n_pg, PAGE_SIZE, N_HEADS, HEAD_DIM),
        svp.reshape(1, bp, n_pg, PAGE_SIZE, N_HEADS, HEAD_DIM),
        gdn_p[None],
        conv_p[None],
        sks.reshape(1, db, 1, N_HEADS, HEAD_DIM),
        svs.reshape(1, db, 1, N_HEADS, HEAD_DIM),
        gdn_s[None],
        conv_s[None],
    )
```

```python
import functools

import jax
import jax.numpy as jnp
from jax import lax
from jax.experimental import pallas as pl
from jax.experimental.pallas import tpu as pltpu

F32 = jnp.float32
BF16 = jnp.bfloat16

LANES = 128
HEAD_DIM = 128
N_HEADS = 8
HEADS_WIDTH = N_HEADS * HEAD_DIM
CONV_W = 4
CONV_DIM = 3 * HEADS_WIDTH
GDN_CHUNK = 64
PAIR = 2 * GDN_CHUNK
PAGE_SIZE = 128
LN_EPS = 1e-5
RMS_EPS = 1e-6
DN_ALPHA = 2.0 ** 0.25
VMEM_LIMIT = 52 * 1024 * 1024
HEAD_GROUP = 8
LOG2E = 1.4426950408889634
SB_QSCALE = HEAD_DIM ** -0.5 * LOG2E

_HI = lax.Precision.HIGHEST


def _cparams(n_axes):
    return pltpu.CompilerParams(dimension_semantics=("arbitrary",) * n_axes,
                                vmem_limit_bytes=VMEM_LIMIT)


def _softplus(x):
    return jnp.maximum(x, 0.0) + jnp.log1p(jnp.exp(-jnp.abs(x)))


def _silu(x):
    h = 0.5 * x
    return h + h * jnp.tanh(h)


def _neg_softplus2(z2):
    return jnp.minimum(-z2, 0.0) - jnp.log2(1.0 + jnp.exp2(-jnp.abs(z2)))


def _dot(a, b):
    return jnp.dot(a, b, preferred_element_type=F32)


def _dot_nt(a, b):
    return lax.dot_general(a, b, (((1,), (1,)), ((), ())), preferred_element_type=F32)


def _dot_tn(a, b):
    return lax.dot_general(a, b, (((0,), (0,)), ((), ())), preferred_element_type=F32)


def _dot_hi(a, b):
    return jnp.dot(a, b, preferred_element_type=F32, precision=_HI)


def _layernorm(pre, g, b):
    mu = jnp.mean(pre, axis=-1, keepdims=True)
    d = pre - mu
    var = jnp.mean(d * d, axis=-1, keepdims=True)
    return d * lax.rsqrt(var + LN_EPS) * g + b


def _mm_kernel(x_ref, w_ref, *o_refs, scale):
    acc = _dot(x_ref[...].astype(BF16), w_ref[...])
    if scale is not None:
        acc = acc * scale
    for o_ref in o_refs:
        o_ref[...] = acc.astype(o_ref.dtype)


def _matmul(x, w, *, tm, tn, name, out_dtypes=(F32,), scale=None):
    m, k = x.shape
    n = w.shape[1]
    outs = pl.pallas_call(
        functools.partial(_mm_kernel, scale=scale),
        grid=(n // tn, m // tm),
        in_specs=[pl.BlockSpec((tm, k), lambda j, i: (i, 0)),
                  pl.BlockSpec((k, tn), lambda j, i: (0, j))],
        out_specs=[pl.BlockSpec((tm, tn), lambda j, i: (i, j)) for _ in out_dtypes],
        out_shape=[jax.ShapeDtypeStruct((m, n), dt) for dt in out_dtypes],
        compiler_params=_cparams(2),
        name=name,
    )(x, w)
    return outs if len(outs) > 1 else outs[0]


def _gdn_prompt_kernel(conv_ref, z_ref, ba_ref, cw_ref, alog_ref, dtb_ref, nw_ref,
                       o_ref, st_ref,
                       ext_ref, q_s, k_s, v_s, beta_s, g_s, *, blk):
    j = pl.program_id(1)

    @pl.when(j == 0)
    def _():
        ext_ref[0:8, :] = jnp.zeros((8, CONV_DIM), F32)
        st_ref[...] = jnp.zeros_like(st_ref)

    @pl.when(j > 0)
    def _():
        ext_ref[0:8, :] = ext_ref[blk:blk + 8, :]

    ext_ref[8:8 + blk, :] = conv_ref[...]

    for c in range(CONV_DIM // LANES):
        cs = slice(c * LANES, (c + 1) * LANES)
        acc = ext_ref[5:5 + blk, cs] * cw_ref[0:1, cs]
        acc = acc + ext_ref[6:6 + blk, cs] * cw_ref[1:2, cs]
        acc = acc + ext_ref[7:7 + blk, cs] * cw_ref[2:3, cs]
        acc = acc + ext_ref[8:8 + blk, cs] * cw_ref[3:4, cs]
        cu = _silu(acc)
        if c < 2 * N_HEADS:
            ss = jnp.sum(cu * cu, axis=-1, keepdims=True)
            cu = cu * lax.rsqrt(ss + RMS_EPS)
        if c < N_HEADS:
            q_s[:, cs] = cu * (HEAD_DIM ** -0.5)
        elif c < 2 * N_HEADS:
            k_s[:, (c - N_HEADS) * LANES:(c - N_HEADS + 1) * LANES] = cu
        else:
            v_s[:, (c - 2 * N_HEADS) * LANES:(c - 2 * N_HEADS + 1) * LANES] = cu

    ba = ba_ref[...]
    beta_s[...] = jax.nn.sigmoid(ba)
    g_s[...] = -jnp.exp(alog_ref[...]) * _softplus(ba + dtb_ref[...])

    row = lax.broadcasted_iota(jnp.int32, (PAIR, PAIR), 0)
    col = lax.broadcasted_iota(jnp.int32, (PAIR, PAIR), 1)
    same = (row // GDN_CHUNK) == (col // GDN_CHUNK)
    tri_incl = same & (row >= col)
    tri_strict = same & (row > col)
    cum_l = tri_incl.astype(F32)
    nw = nw_ref[...]
    half_rows = lax.broadcasted_iota(jnp.int32, (PAIR, 1), 0) // GDN_CHUNK
    halves = [slice(half * GDN_CHUNK, (half + 1) * GDN_CHUNK) for half in range(2)]

    def pair_body(p, carry):
        r0 = pl.multiple_of(p * PAIR, PAIR)
        rows = pl.ds(r0, PAIR)
        g_p = g_s[rows, :]
        beta_p = beta_s[rows, :]
        gc_all = _dot_hi(cum_l, g_p)
        gct_all = jnp.transpose(gc_all)
        for h0 in range(0, N_HEADS, HEAD_GROUP):
            heads = range(h0, h0 + HEAD_GROUP)
            hsl = {h: slice(h * HEAD_DIM, (h + 1) * HEAD_DIM) for h in heads}
            kh = {h: k_s[rows, hsl[h]] for h in heads}
            gcol = {h: gc_all[:, N_HEADS + h:N_HEADS + h + 1] for h in heads}
            gexp = {h: jnp.exp(gcol[h]) for h in heads}
            khb = {h: kh[h].astype(BF16) for h in heads}
            kb = {h: kh[h] * beta_p[:, h:h + 1] for h in heads}
            decay, xp, tm = {}, {}, {}
            for h in heads:
                grow = gct_all[N_HEADS + h:N_HEADS + h + 1, :]
                decay[h] = jnp.exp(jnp.where(tri_incl, gcol[h] - grow, -jnp.inf))
                lower = jnp.where(tri_strict, _dot_nt(kb[h].astype(BF16), khb[h]) * decay[h], 0.0)
                xp[h] = -lower
                tm[h] = xp[h]
            for _ in range(5):
                for h in heads:
                    xb = xp[h].astype(BF16)
                    xp[h] = _dot(xb, xb)
                    tm[h] = tm[h] + xp[h] + _dot(tm[h].astype(BF16), xp[h].astype(BF16))
            u, kcb, qk, qg = {}, {}, {}, {}
            for h in heads:
                rhs = jnp.concatenate([v_s[rows, hsl[h]] * beta_p[:, h:h + 1], kb[h] * gexp[h]], axis=1)
                uk = rhs + _dot(tm[h].astype(BF16), rhs.astype(BF16))
                u[h] = uk[:, :HEAD_DIM]
                kcb[h] = uk[:, HEAD_DIM:].astype(BF16)
                qh = q_s[rows, hsl[h]]
                qk[h] = jnp.where(tri_incl, _dot_nt(qh.astype(BF16), khb[h]) * decay[h], 0.0).astype(BF16)
                qg[h] = (qh * gexp[h]).astype(BF16)
            st = {h: st_ref[h] for h in heads}
            o_halves = {h: [] for h in heads}
            for half in range(2):
                hr = halves[half]
                in_half = half_rows == half
                for h in heads:
                    stb = st[h].astype(BF16)
                    ks_qs = _dot(jnp.concatenate([kcb[h][hr], qg[h][hr]], axis=0), stb)
                    v_new = u[h][hr] - ks_qs[:GDN_CHUNK]
                    v_full = jnp.where(in_half, jnp.concatenate([v_new, v_new], axis=0), 0.0).astype(BF16)
                    o_halves[h].append(ks_qs[GDN_CHUNK:] + _dot(qk[h][hr], v_full))
                    g_last = gcol[h][(half + 1) * GDN_CHUNK - 1:(half + 1) * GDN_CHUNK, :]
                    kdec = jnp.where(in_half, kh[h] * jnp.exp(g_last - gcol[h]), 0.0).astype(BF16)
                    st[h] = st[h] * jnp.exp(g_last) + _dot_tn(kdec, v_full)
            for h in heads:
                st_ref[h] = st[h]
                o = jnp.concatenate(o_halves[h], axis=0)
                o = o * lax.rsqrt(jnp.mean(o * o, axis=-1, keepdims=True) + RMS_EPS) * nw
                zh = z_ref[rows, hsl[h]]
                o_ref[rows, hsl[h]] = (o * _silu(zh)).astype(o_ref.dtype)
        return carry

    lax.fori_loop(0, blk // PAIR, pair_body, 0, unroll=True)


def _gdn_prompt(p1, ba, conv_w, alog_row, dtb_row, nw_row, *, batch, seq, blk):
    nblk = seq // blk
    z_col = CONV_DIM // HEADS_WIDTH
    kern = functools.partial(_gdn_prompt_kernel, blk=blk)
    return pl.pallas_call(
        kern,
        grid=(batch, nblk),
        in_specs=[
            pl.BlockSpec((blk, CONV_DIM), lambda b, j: (b * nblk + j, 0)),
            pl.BlockSpec((blk, HEADS_WIDTH), lambda b, j: (b * nblk + j, z_col)),
            pl.BlockSpec((blk, LANES), lambda b, j: (b * nblk + j, 0)),
            pl.BlockSpec((CONV_W, CONV_DIM), lambda b, j: (0, 0)),
            pl.BlockSpec((1, LANES), lambda b, j: (0, 0)),
            pl.BlockSpec((1, LANES), lambda b, j: (0, 0)),
            pl.BlockSpec((1, HEAD_DIM), lambda b, j: (0, 0)),
        ],
        out_specs=[
            pl.BlockSpec((blk, HEADS_WIDTH), lambda b, j: (b * nblk + j, 0)),
            pl.BlockSpec((None, N_HEADS, HEAD_DIM, HEAD_DIM), lambda b, j: (b, 0, 0, 0)),
        ],
        out_shape=[
            jax.ShapeDtypeStruct((batch * seq, HEADS_WIDTH), BF16),
            jax.ShapeDtypeStruct((batch, N_HEADS, HEAD_DIM, HEAD_DIM), F32),
        ],
        scratch_shapes=[
            pltpu.VMEM((blk + 8, CONV_DIM), F32),
            pltpu.VMEM((blk, HEADS_WIDTH), F32),
            pltpu.VMEM((blk, HEADS_WIDTH), F32),
            pltpu.VMEM((blk, HEADS_WIDTH), F32),
            pltpu.VMEM((blk, LANES), F32),
            pltpu.VMEM((blk, LANES), F32),
        ],
        compiler_params=_cparams(2),
        name="gdn_prompt",
    )(p1, p1, ba, conv_w, alog_row, dtb_row, nw_row)


def _gdn_sample_kernel(u_ref, z_ref, ba_ref, c0_ref, s0_ref, cw_ref, alog_ref, dtb_ref, nw_ref,
                       o_ref, st_ref, cn_ref):
    u = u_ref[0]
    c0 = c0_ref[...]
    acc = c0[0:1] * cw_ref[0:1, :]
    acc = acc + c0[1:2] * cw_ref[1:2, :]
    acc = acc + c0[2:3] * cw_ref[2:3, :]
    acc = acc + u * cw_ref[3:4, :]
    cu = _silu(acc)
    cn_ref[0:1, :] = c0[1:2]
    cn_ref[1:2, :] = c0[2:3]
    cn_ref[2:3, :] = u

    ba = ba_ref[0]
    beta_all = jax.nn.sigmoid(ba)
    g_all = -jnp.exp(alog_ref[...]) * _softplus(ba + dtb_ref[...])
    z = z_ref[0]
    nw = nw_ref[...]
    first_row = lax.broadcasted_iota(jnp.int32, (8, HEAD_DIM), 0) == 0
    for h in range(N_HEADS):
        hs = slice(h * HEAD_DIM, (h + 1) * HEAD_DIM)
        q = cu[:, hs]
        k = cu[:, HEADS_WIDTH + h * HEAD_DIM:HEADS_WIDTH + (h + 1) * HEAD_DIM]
        v = cu[:, 2 * HEADS_WIDTH + h * HEAD_DIM:2 * HEADS_WIDTH + (h + 1) * HEAD_DIM]
        q = q * lax.rsqrt(jnp.sum(q * q, axis=-1, keepdims=True) + RMS_EPS) * (HEAD_DIM ** -0.5)
        k = k * lax.rsqrt(jnp.sum(k * k, axis=-1, keepdims=True) + RMS_EPS)
        beta = beta_all[:, h:h + 1]
        g = g_all[:, N_HEADS + h:N_HEADS + h + 1]
        eg = jnp.exp(g)
        st = s0_ref[h]
        stb = st.astype(BF16)
        kcum = jnp.broadcast_to(k * beta * eg, (8, HEAD_DIM)).astype(BF16)
        qg = jnp.broadcast_to(q * eg, (8, HEAD_DIM)).astype(BF16)
        v_new = v * beta - _dot(kcum, stb)[0:1]
        qk = jnp.sum(q.astype(BF16).astype(F32) * k.astype(BF16).astype(F32), axis=-1, keepdims=True)
        o = _dot(qg, stb)[0:1] + qk * v_new.astype(BF16).astype(F32)
        k8 = jnp.where(first_row, jnp.broadcast_to(k, (8, HEAD_DIM)), 0.0).astype(BF16)
        v8 = jnp.broadcast_to(v_new, (8, HEAD_DIM)).astype(BF16)
        st_ref[h] = st * eg + _dot_tn(k8, v8)
        o = o * lax.rsqrt(jnp.mean(o * o, axis=-1, keepdims=True) + RMS_EPS) * nw
        zh = z[:, hs]
        o_ref[0, :, hs] = (o * _silu(zh)).astype(o_ref.dtype)


def _gdn_sample(conv_in, z, ba, conv0, s0, conv_w, alog_row, dtb_row, nw_row):
    db = conv_in.shape[0]
    row3 = lambda a: a.reshape(db, 1, a.shape[-1])
    full = lambda shape: pl.BlockSpec(shape, lambda b: (0,) * len(shape))
    return pl.pallas_call(
        _gdn_sample_kernel,
        grid=(db,),
        in_specs=[
            pl.BlockSpec((1, 1, CONV_DIM), lambda b: (b, 0, 0)),
            pl.BlockSpec((1, 1, HEADS_WIDTH), lambda b: (b, 0, 0)),
            pl.BlockSpec((1, 1, LANES), lambda b: (b, 0, 0)),
            pl.BlockSpec((None, CONV_W - 1, CONV_DIM), lambda b: (b, 0, 0)),
            pl.BlockSpec((None, N_HEADS, HEAD_DIM, HEAD_DIM), lambda b: (b, 0, 0, 0)),
            full((CONV_W, CONV_DIM)),
            full((1, LANES)),
            full((1, LANES)),
            full((1, HEAD_DIM)),
        ],
        out_specs=[
            pl.BlockSpec((1, 1, HEADS_WIDTH), lambda b: (b, 0, 0)),
            pl.BlockSpec((None, N_HEADS, HEAD_DIM, HEAD_DIM), lambda b: (b, 0, 0, 0)),
            pl.BlockSpec((None, CONV_W - 1, CONV_DIM), lambda b: (b, 0, 0)),
        ],
        out_shape=[
            jax.ShapeDtypeStruct((db, 1, HEADS_WIDTH), F32),
            jax.ShapeDtypeStruct((db, N_HEADS, HEAD_DIM, HEAD_DIM), F32),
            jax.ShapeDtypeStruct((db, CONV_W - 1, CONV_DIM), F32),
        ],
        compiler_params=_cparams(1),
        name="gdn_sample",
    )(row3(conv_in), row3(z), row3(ba), conv0, s0, conv_w, alog_row, dtb_row, nw_row)


def _suffix_matrix():
    j = lax.broadcasted_iota(jnp.int32, (LANES, 2 * LANES), 0)
    s = lax.broadcasted_iota(jnp.int32, (LANES, 2 * LANES), 1)
    return ((j >= s) | (s >= LANES)).astype(BF16)


def _suffix_sums(m, uj):
    cs = _dot(m.astype(BF16), uj)
    return cs[:, :LANES], cs[:, LANES:]


def _sb_prompt_kernel(bias_ref, q_ref, k_ref, v_ref, nw_ref, o_ref, acc_ref, r_ref, *, tq):
    h = pl.program_id(1)
    i = pl.program_id(2)
    bias2 = bias_ref[h] * LOG2E
    uj = _suffix_matrix()
    nsub = tq // LANES
    acc_ref[...] = jnp.zeros_like(acc_ref)
    r_ref[...] = jnp.zeros_like(r_ref)

    for c in reversed(range(nsub)):
        rs = slice(c * LANES, tq)
        nrow = tq - c * LANES
        ks = pl.ds(pl.multiple_of(i * tq + c * LANES, LANES), LANES)
        z = _dot_nt(q_ref[rs, :], k_ref[ks, :]) + bias2
        valid = (lax.broadcasted_iota(jnp.int32, (nrow, LANES), 1)
                 < lax.broadcasted_iota(jnp.int32, (nrow, LANES), 0))
        m = jnp.where(valid, _neg_softplus2(z), 0.0)
        incl, tot = _suffix_sums(m, uj)
        a = jnp.where(valid, jnp.exp2(z + incl + r_ref[rs, :]), 0.0)
        acc_ref[rs, :] += _dot(a.astype(BF16), v_ref[ks, :])
        r_ref[rs, :] += tot

    def body(n, carry):
        ks = pl.ds(pl.multiple_of((i - 1 - n) * tq, tq), tq)
        z = _dot_nt(q_ref[...], k_ref[ks, :]) + bias2
        m = _neg_softplus2(z)
        run = r_ref[...]
        parts = [None] * nsub
        for c in reversed(range(nsub)):
            cs = slice(c * LANES, (c + 1) * LANES)
            incl, tot = _suffix_sums(m[:, cs], uj)
            parts[c] = jnp.exp2(z[:, cs] + incl + run).astype(BF16)
            run = run + tot
        r_ref[...] = run
        acc_ref[...] += _dot(jnp.concatenate(parts, axis=1), v_ref[ks, :])
        return carry

    lax.fori_loop(0, i, body, 0)

    o = acc_ref[...]
    o = o * lax.rsqrt(jnp.mean(o * o, axis=-1, keepdims=True) + RMS_EPS) * nw_ref[pl.ds(h, 1), :]
    o_ref[...] = o.astype(o_ref.dtype)


def _sb_prompt(q, k, v, sb_bias, sb_norm_w, *, batch, seq, tq):
    nq = seq // tq
    kern = functools.partial(_sb_prompt_kernel, tq=tq)
    grid_spec = pltpu.PrefetchScalarGridSpec(
        num_scalar_prefetch=1,
        grid=(batch, N_HEADS, nq),
        in_specs=[
            pl.BlockSpec((tq, HEAD_DIM), lambda b, h, i, bias: (b * nq + i, h)),
            pl.BlockSpec((seq, HEAD_DIM), lambda b, h, i, bias: (b, h)),
            pl.BlockSpec((seq, HEAD_DIM), lambda b, h, i, bias: (b, h)),
            pl.BlockSpec((N_HEADS, HEAD_DIM), lambda b, h, i, bias: (0, 0)),
        ],
        out_specs=pl.BlockSpec((tq, HEAD_DIM), lambda b, h, i, bias: (b * nq + i, h)),
        scratch_shapes=[pltpu.VMEM((tq, HEAD_DIM), F32), pltpu.VMEM((tq, LANES), F32)],
    )
    return pl.pallas_call(
        kern,
        grid_spec=grid_spec,
        out_shape=jax.ShapeDtypeStruct((batch * seq, HEADS_WIDTH), BF16),
        compiler_params=_cparams(3),
        name="sb_prompt",
    )(sb_bias, q, k, v, sb_norm_w)


def _sb_sample_kernel(pt_ref, q_ref, bias_ref, nw_ref, *refs, pages_per_step):
    k_refs = refs[:pages_per_step]
    v_refs = refs[pages_per_step:2 * pages_per_step]
    o_ref, acc_ref, r_ref = refs[2 * pages_per_step:]
    i = pl.program_id(1)

    @pl.when(i == 0)
    def _():
        acc_ref[...] = jnp.zeros_like(acc_ref)
        r_ref[...] = jnp.zeros_like(r_ref)

    flat = PAGE_SIZE * N_HEADS
    nchunk = flat // LANES
    sub = lax.broadcasted_iota(jnp.int32, (N_HEADS, LANES), 0)
    lane = lax.broadcasted_iota(jnp.int32, (N_HEADS, LANES), 1)
    own = (lane % N_HEADS) == sub
    wl = lax.broadcasted_iota(jnp.int32, (LANES, 2 * LANES), 0)
    wc = lax.broadcasted_iota(jnp.int32, (LANES, 2 * LANES), 1)
    wsuf = ((wl // N_HEADS >= wc // N_HEADS) | (wc >= LANES)).astype(BF16)
    qb = q_ref[...].astype(BF16)
    bias2 = bias_ref[...] * LOG2E
    zc = []
    for j in range(pages_per_step):
        kf = k_refs[j][...].reshape(flat, HEAD_DIM).astype(BF16)
        z = _dot_nt(qb, kf)
        zc += [z[:, c * LANES:(c + 1) * LANES] + bias2 for c in range(nchunk)]
    m = jnp.concatenate([jnp.where(own, _neg_softplus2(zz), 0.0) for zz in zc], axis=0)
    m_hi = m.astype(BF16)
    m_lo = (m - m_hi.astype(F32)).astype(BF16)
    cs = _dot(m_hi, wsuf) + _dot(m_lo, wsuf)
    run = r_ref[...]
    acc = acc_ref[...]
    for j in range(pages_per_step):
        parts = [None] * nchunk
        for c in reversed(range(nchunk)):
            n = j * nchunk + c
            rs = slice(n * N_HEADS, (n + 1) * N_HEADS)
            e = zc[n] + cs[rs, :LANES] + run
            parts[c] = jnp.where(own, jnp.exp2(e), 0.0).astype(BF16)
            run = run + cs[rs, LANES:]
        vf = v_refs[j][...].reshape(flat, HEAD_DIM).astype(BF16)
        acc = acc + _dot(jnp.concatenate(parts, axis=1), vf)
    r_ref[...] = run
    acc_ref[...] = acc

    @pl.when(i == pl.num_programs(1) - 1)
    def _():
        o = acc_ref[...]
        o = o * lax.rsqrt(jnp.mean(o * o, axis=-1, keepdims=True) + RMS_EPS) * nw_ref[...]
        o_ref[...] = o.astype(o_ref.dtype)


def _sb_sample(q, cache_k, cache_v, page_table, bias_tile, sb_norm_w, *, pages_per_step):
    db, n_pages = page_table.shape
    n_steps = n_pages // pages_per_step
    kern = functools.partial(_sb_sample_kernel, pages_per_step=pages_per_step)

    def page_spec(j):
        return pl.BlockSpec((None, None, PAGE_SIZE, N_HEADS, HEAD_DIM),
                            lambda b, i, pt: (0, pt[b, n_pages - 1 - (i * pages_per_step + j)], 0, 0, 0))

    grid_spec = pltpu.PrefetchScalarGridSpec(
        num_scalar_prefetch=1,
        grid=(db, n_steps),
        in_specs=[
            pl.BlockSpec((None, N_HEADS, HEAD_DIM), lambda b, i, pt: (b, 0, 0)),
            pl.BlockSpec((N_HEADS, LANES), lambda b, i, pt: (0, 0)),
            pl.BlockSpec((N_HEADS, HEAD_DIM), lambda b, i, pt: (0, 0)),
        ] + [page_spec(j) for j in range(pages_per_step)] * 2,
        out_specs=pl.BlockSpec((None, N_HEADS, HEAD_DIM), lambda b, i, pt: (b, 0, 0)),
        scratch_shapes=[pltpu.VMEM((N_HEADS, HEAD_DIM), F32), pltpu.VMEM((N_HEADS, LANES), F32)],
    )
    return pl.pallas_call(
        kern,
        grid_spec=grid_spec,
        out_shape=jax.ShapeDtypeStruct((db, N_HEADS, HEAD_DIM), F32),
        compiler_params=_cparams(2),
        name="sb_sample",
    )(page_table, q, bias_tile, sb_norm_w,
      *([cache_k] * pages_per_step), *([cache_v] * pages_per_step))


def _out_ln_kernel(og_ref, os_ref, x_ref, wg_ref, ws_ref, g_ref, b_ref, o_ref):
    mixed = _dot(og_ref[...].astype(BF16), wg_ref[...]) + _dot(os_ref[...].astype(BF16), ws_ref[...])
    o_ref[...] = _layernorm(DN_ALPHA * x_ref[...] + mixed, g_ref[...], b_ref[...])


def _out_ln(og, osb, x, wg, ws, g, b, *, tm, name):
    m, d = x.shape
    return pl.pallas_call(
        _out_ln_kernel,
        grid=(m // tm,),
        in_specs=[
            pl.BlockSpec((tm, HEADS_WIDTH), lambda i: (i, 0)),
            pl.BlockSpec((tm, HEADS_WIDTH), lambda i: (i, 0)),
            pl.BlockSpec((tm, d), lambda i: (i, 0)),
            pl.BlockSpec((HEADS_WIDTH, d), lambda i: (0, 0)),
            pl.BlockSpec((HEADS_WIDTH, d), lambda i: (0, 0)),
            pl.BlockSpec((1, d), lambda i: (0, 0)),
            pl.BlockSpec((1, d), lambda i: (0, 0)),
        ],
        out_specs=pl.BlockSpec((tm, d), lambda i: (i, 0)),
        out_shape=jax.ShapeDtypeStruct((m, d), F32),
        compiler_params=_cparams(1),
        name=name,
    )(og, osb, x, wg, ws, g, b)


def _ffn_kernel(h_ref, wu_ref, wd_ref, g_ref, b_ref, o_ref, hb_ref):
    f = pl.program_id(1)

    @pl.when(f == 0)
    def _():
        hb_ref[...] = h_ref[...].astype(BF16)
        o_ref[...] = jnp.zeros_like(o_ref)

    a = jnp.maximum(_dot(hb_ref[...], wu_ref[...]), 0.0)
    o_ref[...] += _dot((a * a).astype(BF16), wd_ref[...])

    @pl.when(f == pl.num_programs(1) - 1)
    def _():
        o_ref[...] = _layernorm(DN_ALPHA * h_ref[...] + o_ref[...], g_ref[...], b_ref[...])


def _ffn(h, wu, wd, g, b, *, tm, tf, name):
    m, d = h.shape
    d_ff = wu.shape[1]
    return pl.pallas_call(
        _ffn_kernel,
        grid=(m // tm, d_ff // tf),
        in_specs=[
            pl.BlockSpec((tm, d), lambda i, f: (i, 0)),
            pl.BlockSpec((d, tf), lambda i, f: (0, f)),
            pl.BlockSpec((tf, d), lambda i, f: (f, 0)),
            pl.BlockSpec((1, d), lambda i, f: (0, 0)),
            pl.BlockSpec((1, d), lambda i, f: (0, 0)),
        ],
        out_specs=pl.BlockSpec((tm, d), lambda i, f: (i, 0)),
        out_shape=jax.ShapeDtypeStruct((m, d), F32),
        scratch_shapes=[pltpu.VMEM((tm, d), BF16)],
        compiler_params=_cparams(2),
        name=name,
    )(h, wu, wd, g, b)


def kernel(x_prompt, x_sample, cache_k, cache_v, state_gdn, state_conv, page_table, w_in, conv_w, a_log,
           dt_bias, gdn_norm_w, sb_norm_w, sb_bias, w_out, ln1_g, ln1_b, w_up, w_down, ln2_g, ln2_b):
    bp, sp, d = x_prompt.shape
    db = x_sample.shape[0]
    assert w_in.shape[0] == 1 and x_sample.shape[1] == 1

    w = w_in[0]
    o_z = CONV_DIM
    o_b = o_z + HEADS_WIDTH
    o_a = o_b + N_HEADS
    o_q = o_a + N_HEADS
    o_k = o_q + HEADS_WIDTH
    o_v = o_k + HEADS_WIDTH
    w_main = w[:, :o_b].astype(BF16)
    w_q = w[:, o_q:o_k].astype(BF16)
    w_k = w[:, o_k:o_v].astype(BF16)
    w_v = w[:, o_v:].astype(BF16)
    w_ba = jnp.pad(w[:, o_b:o_q], ((0, 0), (0, LANES - 2 * N_HEADS))).astype(BF16)
    w_og = w_out[0, :HEADS_WIDTH].astype(BF16)
    w_os = w_out[0, HEADS_WIDTH:].astype(BF16)
    wu = w_up[0].astype(BF16)
    wd = w_down[0].astype(BF16)
    pad_row = lambda v: jnp.pad(v, (N_HEADS, LANES - 2 * N_HEADS)).reshape(1, LANES)
    alog_row = pad_row(a_log[0])
    dtb_row = pad_row(dt_bias[0])
    nw_row = gdn_norm_w[0].reshape(1, HEAD_DIM)
    cw = conv_w[0]
    row = lambda v: v.reshape(1, d)
    bias_tile = jnp.broadcast_to(sb_bias[0][:, None], (N_HEADS, LANES))

    xp = x_prompt.reshape(bp * sp, d)
    p1 = _matmul(xp, w_main, tm=512, tn=1024, name="proj_main_p")
    sqp = _matmul(xp, w_q, tm=512, tn=1024, name="proj_q_p", out_dtypes=(BF16,), scale=SB_QSCALE)
    skp, skp_b = _matmul(xp, w_k, tm=512, tn=1024, name="proj_k_p", out_dtypes=(F32, BF16))
    svp, svp_b = _matmul(xp, w_v, tm=512, tn=1024, name="proj_v_p", out_dtypes=(F32, BF16))
    bap = _matmul(xp, w_ba, tm=512, tn=LANES, name="proj_ba_p")
    og_p, gdn_p = _gdn_prompt(p1, bap, cw, alog_row, dtb_row, nw_row, batch=bp, seq=sp, blk=256)
    os_p = _sb_prompt(sqp, skp_b, svp_b, sb_bias[0], sb_norm_w[0], batch=bp, seq=sp, tq=512)
    h_p = _out_ln(og_p, os_p, xp, w_og, w_os, row(ln1_g[0]), row(ln1_b[0]), tm=512, name="out_ln_p")
    y_p = _ffn(h_p, wu, wd, row(ln2_g[0]), row(ln2_b[0]), tm=512, tf=512, name="ffn_p")
    conv_p = p1.reshape(bp, sp, -1)[:, sp - (CONV_W - 1):, :CONV_DIM]

    xs = x_sample.reshape(db, d)
    s1 = _matmul(xs, w_main, tm=db, tn=1024, name="proj_main_s")
    sqs = _matmul(xs, w_q, tm=db, tn=1024, name="proj_q_s", scale=SB_QSCALE)
    sks = _matmul(xs, w_k, tm=db, tn=1024, name="proj_k_s")
    svs = _matmul(xs, w_v, tm=db, tn=1024, name="proj_v_s")
    bas = _matmul(xs, w_ba, tm=db, tn=LANES, name="proj_ba_s")
    og_s, gdn_s, conv_s = _gdn_sample(s1[:, :CONV_DIM], s1[:, CONV_DIM:], bas,
                                      state_conv[0], state_gdn[0], cw, alog_row, dtb_row, nw_row)
    os_s = _sb_sample(sqs.reshape(db, N_HEADS, HEAD_DIM), cache_k, cache_v,
                      page_table, bias_tile, sb_norm_w[0], pages_per_step=8)
    h_s = _out_ln(og_s.reshape(db, HEADS_WIDTH), os_s.reshape(db, HEADS_WIDTH), xs, w_og, w_os,
                  row(ln1_g[0]), row(ln1_b[0]), tm=db, name="out_ln_s")
    y_s = _ffn(h_s, wu, wd, row(ln2_g[0]), row(ln2_b[0]), tm=db, tf=512, name="ffn_s")

    n_pg = sp // PAGE_SIZE
    return (
        y_p.reshape(bp, sp, d),
        y_s.reshape(db, 1, d),
        skp.reshape(1, bp, n_pg, PAGE_SIZE, N_HEADS, HEAD_DIM),
        svp.reshape(1, bp, n_pg, PAGE_SIZE, N_HEADS, HEAD_DIM),
        gdn_p[None],
        conv_p[None],
        sks.reshape(1, db, 1, N_HEADS, HEAD_DIM),
        svs.reshape(1, db, 1, N_HEADS, HEAD_DIM),
        gdn_s[None],
        conv_s[None],
    )
```

```python
import functools

import jax
import jax.numpy as jnp
from jax import lax
from jax.experimental import pallas as pl
from jax.experimental.pallas import tpu as pltpu

F32 = jnp.float32
BF16 = jnp.bfloat16

LANES = 128
HEAD_DIM = 128
N_HEADS = 8
HEADS_WIDTH = N_HEADS * HEAD_DIM
CONV_W = 4
CONV_DIM = 3 * HEADS_WIDTH
GDN_CHUNK = 64
PAIR = 2 * GDN_CHUNK
PAGE_SIZE = 128
LN_EPS = 1e-5
RMS_EPS = 1e-6
DN_ALPHA = 2.0 ** 0.25
VMEM_LIMIT = 52 * 1024 * 1024
HEAD_GROUP = 8
LOG2E = 1.4426950408889634
SB_QSCALE = HEAD_DIM ** -0.5 * LOG2E

_HI = lax.Precision.HIGHEST


def _cparams(n_axes):
    return pltpu.CompilerParams(dimension_semantics=("arbitrary",) * n_axes,
                                vmem_limit_bytes=VMEM_LIMIT)


def _softplus(x):
    return jnp.maximum(x, 0.0) + jnp.log1p(jnp.exp(-jnp.abs(x)))


def _silu(x):
    h = 0.5 * x
    return h + h * jnp.tanh(h)


def _neg_softplus2(z2):
    return jnp.minimum(-z2, 0.0) - jnp.log2(1.0 + jnp.exp2(-jnp.abs(z2)))


def _dot(a, b):
    return jnp.dot(a, b, preferred_element_type=F32)


def _dot_nt(a, b):
    return lax.dot_general(a, b, (((1,), (1,)), ((), ())), preferred_element_type=F32)


def _dot_tn(a, b):
    return lax.dot_general(a, b, (((0,), (0,)), ((), ())), preferred_element_type=F32)


def _dot_hi(a, b):
    return jnp.dot(a, b, preferred_element_type=F32, precision=_HI)


def _layernorm(pre, g, b):
    mu = jnp.mean(pre, axis=-1, keepdims=True)
    d = pre - mu
    var = jnp.mean(d * d, axis=-1, keepdims=True)
    return d * lax.rsqrt(var + LN_EPS) * g + b


def _mm_kernel(x_ref, w_ref, *o_refs):
    acc = _dot(x_ref[...].astype(BF16), w_ref[...])
    for o_ref in o_refs:
        o_ref[...] = acc.astype(o_ref.dtype)


def _matmul(x, w, *, tm, tn, name, out_dtypes=(F32,)):
    m, k = x.shape
    n = w.shape[1]
    outs = pl.pallas_call(
        _mm_kernel,
        grid=(n // tn, m // tm),
        in_specs=[pl.BlockSpec((tm, k), lambda j, i: (i, 0)),
                  pl.BlockSpec((k, tn), lambda j, i: (0, j))],
        out_specs=[pl.BlockSpec((tm, tn), lambda j, i: (i, j)) for _ in out_dtypes],
        out_shape=[jax.ShapeDtypeStruct((m, n), dt) for dt in out_dtypes],
        compiler_params=_cparams(2),
        name=name,
    )(x, w)
    return outs if len(outs) > 1 else outs[0]


def _mm_split_kernel(x_ref, w_ref, *o_refs, scales):
    acc = _dot(x_ref[...].astype(BF16), w_ref[...])
    col = 0
    for o_ref, scale in zip(o_refs, scales):
        part = acc[:, col:col + o_ref.shape[-1]]
        col += o_ref.shape[-1]
        o_ref[...] = (part if scale is None else part * scale).astype(o_ref.dtype)


def _matmul_split(x, w, *, tm, name, outs):
    m, k = x.shape
    n = w.shape[1]
    assert n == sum(width for width, _, _ in outs)
    return pl.pallas_call(
        functools.partial(_mm_split_kernel, scales=tuple(s for _, _, s in outs)),
        grid=(m // tm,),
        in_specs=[pl.BlockSpec((tm, k), lambda i: (i, 0)),
                  pl.BlockSpec((k, n), lambda i: (0, 0))],
        out_specs=[pl.BlockSpec((tm, width), lambda i: (i, 0)) for width, _, _ in outs],
        out_shape=[jax.ShapeDtypeStruct((m, width), dt) for width, dt, _ in outs],
        compiler_params=_cparams(1),
        name=name,
    )(x, w)


def _gdn_prompt_kernel(conv_ref, z_ref, ba_ref, cw_ref, alog_ref, dtb_ref, nw_ref,
                       o_ref, st_ref,
                       ext_ref, q_s, k_s, v_s, beta_s, g_s, *, blk):
    j = pl.program_id(1)

    @pl.when(j == 0)
    def _():
        ext_ref[0:8, :] = jnp.zeros((8, CONV_DIM), F32)
        st_ref[...] = jnp.zeros_like(st_ref)

    @pl.when(j > 0)
    def _():
        ext_ref[0:8, :] = ext_ref[blk:blk + 8, :]

    ext_ref[8:8 + blk, :] = conv_ref[...]

    for c in range(CONV_DIM // LANES):
        cs = slice(c * LANES, (c + 1) * LANES)
        acc = ext_ref[5:5 + blk, cs] * cw_ref[0:1, cs]
        acc = acc + ext_ref[6:6 + blk, cs] * cw_ref[1:2, cs]
        acc = acc + ext_ref[7:7 + blk, cs] * cw_ref[2:3, cs]
        acc = acc + ext_ref[8:8 + blk, cs] * cw_ref[3:4, cs]
        cu = _silu(acc)
        if c < 2 * N_HEADS:
            ss = jnp.sum(cu * cu, axis=-1, keepdims=True)
            cu = cu * lax.rsqrt(ss + RMS_EPS)
        if c < N_HEADS:
            q_s[:, cs] = cu * (HEAD_DIM ** -0.5)
        elif c < 2 * N_HEADS:
            k_s[:, (c - N_HEADS) * LANES:(c - N_HEADS + 1) * LANES] = cu
        else:
            v_s[:, (c - 2 * N_HEADS) * LANES:(c - 2 * N_HEADS + 1) * LANES] = cu

    ba = ba_ref[...]
    beta_s[...] = jax.nn.sigmoid(ba)
    g_s[...] = -jnp.exp(alog_ref[...]) * _softplus(ba + dtb_ref[...])

    row = lax.broadcasted_iota(jnp.int32, (PAIR, PAIR), 0)
    col = lax.broadcasted_iota(jnp.int32, (PAIR, PAIR), 1)
    same = (row // GDN_CHUNK) == (col // GDN_CHUNK)
    tri_incl = same & (row >= col)
    tri_strict = same & (row > col)
    cum_l = tri_incl.astype(F32)
    nw = nw_ref[...]
    half_rows = lax.broadcasted_iota(jnp.int32, (PAIR, 1), 0) // GDN_CHUNK
    halves = [slice(half * GDN_CHUNK, (half + 1) * GDN_CHUNK) for half in range(2)]

    def pair_body(p, carry):
        r0 = pl.multiple_of(p * PAIR, PAIR)
        rows = pl.ds(r0, PAIR)
        g_p = g_s[rows, :]
        beta_p = beta_s[rows, :]
        gc_all = _dot_hi(cum_l, g_p)
        gct_all = jnp.transpose(gc_all)
        for h0 in range(0, N_HEADS, HEAD_GROUP):
            heads = range(h0, h0 + HEAD_GROUP)
            hsl = {h: slice(h * HEAD_DIM, (h + 1) * HEAD_DIM) for h in heads}
            kh = {h: k_s[rows, hsl[h]] for h in heads}
            gcol = {h: gc_all[:, N_HEADS + h:N_HEADS + h + 1] for h in heads}
            gexp = {h: jnp.exp(gcol[h]) for h in heads}
            khb = {h: kh[h].astype(BF16) for h in heads}
            kb = {h: kh[h] * beta_p[:, h:h + 1] for h in heads}
            decay, xp, tm = {}, {}, {}
            for h in heads:
                grow = gct_all[N_HEADS + h:N_HEADS + h + 1, :]
                decay[h] = jnp.exp(jnp.where(tri_incl, gcol[h] - grow, -jnp.inf))
                lower = jnp.where(tri_strict, _dot_nt(kb[h].astype(BF16), khb[h]) * decay[h], 0.0)
                xp[h] = -lower
                tm[h] = xp[h]
            for _ in range(5):
                for h in heads:
                    xb = xp[h].astype(BF16)
                    xp[h] = _dot(xb, xb)
                    tm[h] = tm[h] + xp[h] + _dot(tm[h].astype(BF16), xp[h].astype(BF16))
            u, kcb, qk, qg = {}, {}, {}, {}
            for h in heads:
                rhs = jnp.concatenate([v_s[rows, hsl[h]] * beta_p[:, h:h + 1], kb[h] * gexp[h]], axis=1)
                uk = rhs + _dot(tm[h].astype(BF16), rhs.astype(BF16))
                u[h] = uk[:, :HEAD_DIM]
                kcb[h] = uk[:, HEAD_DIM:].astype(BF16)
                qh = q_s[rows, hsl[h]]
                qk[h] = jnp.where(tri_incl, _dot_nt(qh.astype(BF16), khb[h]) * decay[h], 0.0).astype(BF16)
                qg[h] = (qh * gexp[h]).astype(BF16)
            st = {h: st_ref[h] for h in heads}
            o_halves = {h: [] for h in heads}
            for half in range(2):
                hr = halves[half]
                in_half = half_rows == half
                for h in heads:
                    stb = st[h].astype(BF16)
                    ks_qs = _dot(jnp.concatenate([kcb[h][hr], qg[h][hr]], axis=0), stb)
                    v_new = u[h][hr] - ks_qs[:GDN_CHUNK]
                    v_full = jnp.where(in_half, jnp.concatenate([v_new, v_new], axis=0), 0.0).astype(BF16)
                    o_halves[h].append(ks_qs[GDN_CHUNK:] + _dot(qk[h][hr], v_full))
                    g_last = gcol[h][(half + 1) * GDN_CHUNK - 1:(half + 1) * GDN_CHUNK, :]
                    kdec = jnp.where(in_half, kh[h] * jnp.exp(g_last - gcol[h]), 0.0).astype(BF16)
                    st[h] = st[h] * jnp.exp(g_last) + _dot_tn(kdec, v_full)
            for h in heads:
                st_ref[h] = st[h]
                o = jnp.concatenate(o_halves[h], axis=0)
                o = o * lax.rsqrt(jnp.mean(o * o, axis=-1, keepdims=True) + RMS_EPS) * nw
                zh = z_ref[rows, hsl[h]]
                o_ref[rows, hsl[h]] = (o * _silu(zh)).astype(o_ref.dtype)
        return carry

    lax.fori_loop(0, blk // PAIR, pair_body, 0, unroll=True)


def _gdn_prompt(p1, ba, conv_w, alog_row, dtb_row, nw_row, *, batch, seq, blk):
    nblk = seq // blk
    z_col = CONV_DIM // HEADS_WIDTH
    kern = functools.partial(_gdn_prompt_kernel, blk=blk)
    return pl.pallas_call(
        kern,
        grid=(batch, nblk),
        in_specs=[
            pl.BlockSpec((blk, CONV_DIM), lambda b, j: (b * nblk + j, 0)),
            pl.BlockSpec((blk, HEADS_WIDTH), lambda b, j: (b * nblk + j, z_col)),
            pl.BlockSpec((blk, LANES), lambda b, j: (b * nblk + j, 0)),
            pl.BlockSpec((CONV_W, CONV_DIM), lambda b, j: (0, 0)),
            pl.BlockSpec((1, LANES), lambda b, j: (0, 0)),
            pl.BlockSpec((1, LANES), lambda b, j: (0, 0)),
            pl.BlockSpec((1, HEAD_DIM), lambda b, j: (0, 0)),
        ],
        out_specs=[
            pl.BlockSpec((blk, HEADS_WIDTH), lambda b, j: (b * nblk + j, 0)),
            pl.BlockSpec((None, N_HEADS, HEAD_DIM, HEAD_DIM), lambda b, j: (b, 0, 0, 0)),
        ],
        out_shape=[
            jax.ShapeDtypeStruct((batch * seq, HEADS_WIDTH), BF16),
            jax.ShapeDtypeStruct((batch, N_HEADS, HEAD_DIM, HEAD_DIM), F32),
        ],
        scratch_shapes=[
            pltpu.VMEM((blk + 8, CONV_DIM), F32),
            pltpu.VMEM((blk, HEADS_WIDTH), F32),
            pltpu.VMEM((blk, HEADS_WIDTH), F32),
            pltpu.VMEM((blk, HEADS_WIDTH), F32),
            pltpu.VMEM((blk, LANES), F32),
            pltpu.VMEM((blk, LANES), F32),
        ],
        compiler_params=_cparams(2),
        name="gdn_prompt",
    )(p1, p1, ba, conv_w, alog_row, dtb_row, nw_row)


def _gdn_sample_kernel(u_ref, z_ref, ba_ref, c0_ref, s0_ref, cw_ref, alog_ref, dtb_ref, nw_ref,
                       o_ref, st_ref, cn_ref):
    u = u_ref[0]
    c0 = c0_ref[...]
    acc = c0[0:1] * cw_ref[0:1, :]
    acc = acc + c0[1:2] * cw_ref[1:2, :]
    acc = acc + c0[2:3] * cw_ref[2:3, :]
    acc = acc + u * cw_ref[3:4, :]
    cu = _silu(acc)
    cn_ref[0:1, :] = c0[1:2]
    cn_ref[1:2, :] = c0[2:3]
    cn_ref[2:3, :] = u

    ba = ba_ref[0]
    beta_all = jax.nn.sigmoid(ba)
    g_all = -jnp.exp(alog_ref[...]) * _softplus(ba + dtb_ref[...])
    z = z_ref[0]
    nw = nw_ref[...]
    first_row = lax.broadcasted_iota(jnp.int32, (8, HEAD_DIM), 0) == 0
    for h in range(N_HEADS):
        hs = slice(h * HEAD_DIM, (h + 1) * HEAD_DIM)
        q = cu[:, hs]
        k = cu[:, HEADS_WIDTH + h * HEAD_DIM:HEADS_WIDTH + (h + 1) * HEAD_DIM]
        v = cu[:, 2 * HEADS_WIDTH + h * HEAD_DIM:2 * HEADS_WIDTH + (h + 1) * HEAD_DIM]
        q = q * lax.rsqrt(jnp.sum(q * q, axis=-1, keepdims=True) + RMS_EPS) * (HEAD_DIM ** -0.5)
        k = k * lax.rsqrt(jnp.sum(k * k, axis=-1, keepdims=True) + RMS_EPS)
        beta = beta_all[:, h:h + 1]
        g = g_all[:, N_HEADS + h:N_HEADS + h + 1]
        eg = jnp.exp(g)
        st = s0_ref[h]
        stb = st.astype(BF16)
        kcum = jnp.broadcast_to(k * beta * eg, (8, HEAD_DIM)).astype(BF16)
        qg = jnp.broadcast_to(q * eg, (8, HEAD_DIM)).astype(BF16)
        v_new = v * beta - _dot(kcum, stb)[0:1]
        qk = jnp.sum(q.astype(BF16).astype(F32) * k.astype(BF16).astype(F32), axis=-1, keepdims=True)
        o = _dot(qg, stb)[0:1] + qk * v_new.astype(BF16).astype(F32)
        k8 = jnp.where(first_row, jnp.broadcast_to(k, (8, HEAD_DIM)), 0.0).astype(BF16)
        v8 = jnp.broadcast_to(v_new, (8, HEAD_DIM)).astype(BF16)
        st_ref[h] = st * eg + _dot_tn(k8, v8)
        o = o * lax.rsqrt(jnp.mean(o * o, axis=-1, keepdims=True) + RMS_EPS) * nw
        zh = z[:, hs]
        o_ref[0, :, hs] = (o * _silu(zh)).astype(o_ref.dtype)


def _gdn_sample(conv_in, z, ba, conv0, s0, conv_w, alog_row, dtb_row, nw_row):
    db = conv_in.shape[0]
    row3 = lambda a: a.reshape(db, 1, a.shape[-1])
    full = lambda shape: pl.BlockSpec(shape, lambda b: (0,) * len(shape))
    return pl.pallas_call(
        _gdn_sample_kernel,
        grid=(db,),
        in_specs=[
            pl.BlockSpec((1, 1, CONV_DIM), lambda b: (b, 0, 0)),
            pl.BlockSpec((1, 1, HEADS_WIDTH), lambda b: (b, 0, 0)),
            pl.BlockSpec((1, 1, LANES), lambda b: (b, 0, 0)),
            pl.BlockSpec((None, CONV_W - 1, CONV_DIM), lambda b: (b, 0, 0)),
            pl.BlockSpec((None, N_HEADS, HEAD_DIM, HEAD_DIM), lambda b: (b, 0, 0, 0)),
            full((CONV_W, CONV_DIM)),
            full((1, LANES)),
            full((1, LANES)),
            full((1, HEAD_DIM)),
        ],
        out_specs=[
            pl.BlockSpec((1, 1, HEADS_WIDTH), lambda b: (b, 0, 0)),
            pl.BlockSpec((None, N_HEADS, HEAD_DIM, HEAD_DIM), lambda b: (b, 0, 0, 0)),
            pl.BlockSpec((None, CONV_W - 1, CONV_DIM), lambda b: (b, 0, 0)),
        ],
        out_shape=[
            jax.ShapeDtypeStruct((db, 1, HEADS_WIDTH), F32),
            jax.ShapeDtypeStruct((db, N_HEADS, HEAD_DIM, HEAD_DIM), F32),
            jax.ShapeDtypeStruct((db, CONV_W - 1, CONV_DIM), F32),
        ],
        compiler_params=_cparams(1),
        name="gdn_sample",
    )(row3(conv_in), row3(z), row3(ba), conv0, s0, conv_w, alog_row, dtb_row, nw_row)


def _suffix_matrix():
    j = lax.broadcasted_iota(jnp.int32, (LANES, 2 * LANES), 0)
    s = lax.broadcasted_iota(jnp.int32, (LANES, 2 * LANES), 1)
    return ((j >= s) | (s >= LANES)).astype(BF16)


def _suffix_sums(m, uj):
    cs = _dot(m.astype(BF16), uj)
    return cs[:, :LANES], cs[:, LANES:]


def _sb_prompt_kernel(bias_ref, q_ref, k_ref, v_ref, nw_ref, o_ref, acc_ref, r_ref, *, tq):
    h = pl.program_id(1)
    i = pl.program_id(2)
    bias2 = bias_ref[h] * LOG2E
    uj = _suffix_matrix()
    nsub = tq // LANES
    acc_ref[...] = jnp.zeros_like(acc_ref)
    r_ref[...] = jnp.zeros_like(r_ref)

    for c in reversed(range(nsub)):
        rs = slice(c * LANES, tq)
        nrow = tq - c * LANES
        ks = pl.ds(pl.multiple_of(i * tq + c * LANES, LANES), LANES)
        z = _dot_nt(q_ref[rs, :], k_ref[ks, :]) + bias2
        valid = (lax.broadcasted_iota(jnp.int32, (nrow, LANES), 1)
                 < lax.broadcasted_iota(jnp.int32, (nrow, LANES), 0))
        m = jnp.where(valid, _neg_softplus2(z), 0.0)
        incl, tot = _suffix_sums(m, uj)
        a = jnp.where(valid, jnp.exp2(z + incl + r_ref[rs, :]), 0.0)
        acc_ref[rs, :] += _dot(a.astype(BF16), v_ref[ks, :])
        r_ref[rs, :] += tot

    def body(n, carry):
        ks = pl.ds(pl.multiple_of((i - 1 - n) * tq, tq), tq)
        z = _dot_nt(q_ref[...], k_ref[ks, :]) + bias2
        m = _neg_softplus2(z)
        run = r_ref[...]
        parts = [None] * nsub
        for c in reversed(range(nsub)):
            cs = slice(c * LANES, (c + 1) * LANES)
            incl, tot = _suffix_sums(m[:, cs], uj)
            parts[c] = jnp.exp2(z[:, cs] + incl + run).astype(BF16)
            run = run + tot
        r_ref[...] = run
        acc_ref[...] += _dot(jnp.concatenate(parts, axis=1), v_ref[ks, :])
        return carry

    lax.fori_loop(0, i, body, 0)

    o = acc_ref[...]
    o = o * lax.rsqrt(jnp.mean(o * o, axis=-1, keepdims=True) + RMS_EPS) * nw_ref[pl.ds(h, 1), :]
    o_ref[...] = o.astype(o_ref.dtype)


def _sb_prompt(q, k, v, sb_bias, sb_norm_w, *, batch, seq, tq):
    nq = seq // tq
    kern = functools.partial(_sb_prompt_kernel, tq=tq)
    grid_spec = pltpu.PrefetchScalarGridSpec(
        num_scalar_prefetch=1,
        grid=(batch, N_HEADS, nq),
        in_specs=[
            pl.BlockSpec((tq, HEAD_DIM), lambda b, h, i, bias: (b * nq + i, h)),
            pl.BlockSpec((seq, HEAD_DIM), lambda b, h, i, bias: (b, h)),
            pl.BlockSpec((seq, HEAD_DIM), lambda b, h, i, bias: (b, h)),
            pl.BlockSpec((N_HEADS, HEAD_DIM), lambda b, h, i, bias: (0, 0)),
        ],
        out_specs=pl.BlockSpec((tq, HEAD_DIM), lambda b, h, i, bias: (b * nq + i, h)),
        scratch_shapes=[pltpu.VMEM((tq, HEAD_DIM), F32), pltpu.VMEM((tq, LANES), F32)],
    )
    return pl.pallas_call(
        kern,
        grid_spec=grid_spec,
        out_shape=jax.ShapeDtypeStruct((batch * seq, HEADS_WIDTH), BF16),
        compiler_params=_cparams(3),
        name="sb_prompt",
    )(sb_bias, q, k, v, sb_norm_w)


_PAGE_ROWS = PAGE_SIZE * N_HEADS
_PAGE_CHUNKS = _PAGE_ROWS // LANES


def _own_head_lanes():
    sub = lax.broadcasted_iota(jnp.int32, (N_HEADS, LANES), 0)
    lane = lax.broadcasted_iota(jnp.int32, (N_HEADS, LANES), 1)
    return (lane % N_HEADS) == sub


def _sb_page_scores(q_ref, bias_ref, k_refs):
    qb = q_ref[...].astype(BF16)
    bias2 = bias_ref[...] * LOG2E
    zc = []
    for k_ref in k_refs:
        z = _dot_nt(qb, k_ref[...].reshape(_PAGE_ROWS, HEAD_DIM).astype(BF16))
        zc += [z[:, c * LANES:(c + 1) * LANES] + bias2 for c in range(_PAGE_CHUNKS)]
    return zc


def _sb_page_suffix(zc):
    own = _own_head_lanes()
    wl = lax.broadcasted_iota(jnp.int32, (LANES, 2 * LANES), 0)
    wc = lax.broadcasted_iota(jnp.int32, (LANES, 2 * LANES), 1)
    wsuf = ((wl // N_HEADS >= wc // N_HEADS) | (wc >= LANES)).astype(BF16)
    m = jnp.concatenate([jnp.where(own, _neg_softplus2(zz), 0.0) for zz in zc], axis=0)
    m_hi = m.astype(BF16)
    m_lo = (m - m_hi.astype(F32)).astype(BF16)
    return _dot(m_hi, wsuf) + _dot(m_lo, wsuf)


def _sb_page_fold(zc, cs, v_refs, acc_ref, r_ref):
    own = _own_head_lanes()
    run = r_ref[...]
    acc = acc_ref[...]
    for j, v_ref in enumerate(v_refs):
        parts = [None] * _PAGE_CHUNKS
        for c in reversed(range(_PAGE_CHUNKS)):
            n = j * _PAGE_CHUNKS + c
            rs = slice(n * N_HEADS, (n + 1) * N_HEADS)
            e = zc[n] + cs[rs, :LANES] + run
            parts[c] = jnp.where(own, jnp.exp2(e), 0.0).astype(BF16)
            run = run + cs[rs, LANES:]
        vf = v_ref[...].reshape(_PAGE_ROWS, HEAD_DIM).astype(BF16)
        acc = acc + _dot(jnp.concatenate(parts, axis=1), vf)
    r_ref[...] = run
    acc_ref[...] = acc


def _out_ln_kernel(og_ref, os_ref, x_ref, wg_ref, ws_ref, g_ref, b_ref, o_ref):
    mixed = _dot(og_ref[...].astype(BF16), wg_ref[...]) + _dot(os_ref[...].astype(BF16), ws_ref[...])
    o_ref[...] = _layernorm(DN_ALPHA * x_ref[...] + mixed, g_ref[...], b_ref[...])


def _out_ln(og, osb, x, wg, ws, g, b, *, tm, name):
    m, d = x.shape
    return pl.pallas_call(
        _out_ln_kernel,
        grid=(m // tm,),
        in_specs=[
            pl.BlockSpec((tm, HEADS_WIDTH), lambda i: (i, 0)),
            pl.BlockSpec((tm, HEADS_WIDTH), lambda i: (i, 0)),
            pl.BlockSpec((tm, d), lambda i: (i, 0)),
            pl.BlockSpec((HEADS_WIDTH, d), lambda i: (0, 0)),
            pl.BlockSpec((HEADS_WIDTH, d), lambda i: (0, 0)),
            pl.BlockSpec((1, d), lambda i: (0, 0)),
            pl.BlockSpec((1, d), lambda i: (0, 0)),
        ],
        out_specs=pl.BlockSpec((tm, d), lambda i: (i, 0)),
        out_shape=jax.ShapeDtypeStruct((m, d), F32),
        compiler_params=_cparams(1),
        name=name,
    )(og, osb, x, wg, ws, g, b)


def _ffn_kernel(*refs, pages_per_step, steps_per_seq):
    if pages_per_step:
        _, h_ref, wu_ref, wd_ref, g_ref, b_ref, q_ref, bias_ref, nw_ref = refs[:9]
        k_refs = refs[9:9 + pages_per_step]
        v_refs = refs[9 + pages_per_step:9 + 2 * pages_per_step]
        o_ref, os_ref, hb_ref, acc_ref, r_ref = refs[9 + 2 * pages_per_step:]
    else:
        h_ref, wu_ref, wd_ref, g_ref, b_ref, o_ref, hb_ref = refs
    f = pl.program_id(1)
    n_f = pl.num_programs(1)

    @pl.when(f == 0)
    def _():
        hb_ref[...] = h_ref[...].astype(BF16)
        o_ref[...] = jnp.zeros_like(o_ref)

    if pages_per_step:
        s = (pl.program_id(0) * n_f + f) % steps_per_seq

        @pl.when(s == 0)
        def _():
            acc_ref[...] = jnp.zeros_like(acc_ref)
            r_ref[...] = jnp.zeros_like(r_ref)

    if pages_per_step:
        zc = _sb_page_scores(q_ref, bias_ref, k_refs)
    a = jnp.maximum(_dot(hb_ref[...], wu_ref[...]), 0.0)
    if pages_per_step:
        cs = _sb_page_suffix(zc)
    o_ref[...] += _dot((a * a).astype(BF16), wd_ref[...])
    if pages_per_step:
        _sb_page_fold(zc, cs, v_refs, acc_ref, r_ref)

        @pl.when(s == steps_per_seq - 1)
        def _():
            o = acc_ref[...]
            o = o * lax.rsqrt(jnp.mean(o * o, axis=-1, keepdims=True) + RMS_EPS) * nw_ref[...]
            os_ref[...] = o.astype(os_ref.dtype)

    @pl.when(f == n_f - 1)
    def _():
        o_ref[...] = _layernorm(DN_ALPHA * h_ref[...] + o_ref[...], g_ref[...], b_ref[...])


def _ffn(h, wu, wd, g, b, *, tm, tf, name, attn=None):
    m, d = h.shape
    d_ff = wu.shape[1]
    n_m, n_f = m // tm, d_ff // tf
    in_specs = [
        pl.BlockSpec((tm, d), lambda i, f, *_: (i, 0)),
        pl.BlockSpec((d, tf), lambda i, f, *_: (0, f)),
        pl.BlockSpec((tf, d), lambda i, f, *_: (f, 0)),
        pl.BlockSpec((1, d), lambda i, f, *_: (0, 0)),
        pl.BlockSpec((1, d), lambda i, f, *_: (0, 0)),
    ]
    out_specs = [pl.BlockSpec((tm, d), lambda i, f, *_: (i, 0))]
    out_shape = [jax.ShapeDtypeStruct((m, d), F32)]
    scratch = [pltpu.VMEM((tm, d), BF16)]
    args = [h, wu, wd, g, b]
    if attn is None:
        pages_per_step = steps_per_seq = 0
        prefetch = []
    else:
        q, cache_k, cache_v, page_table, bias_tile, sb_norm_w, pages_per_step = attn
        db, n_pages = page_table.shape
        steps_per_seq = n_pages // pages_per_step
        assert n_m * n_f == db * steps_per_seq, "attention steps must tile the FFN grid exactly"
        seq_of = lambda i, f: (i * n_f + f) // steps_per_seq
        first_page = lambda i, f: n_pages - 1 - ((i * n_f + f) % steps_per_seq) * pages_per_step

        def page_spec(j):
            return pl.BlockSpec((None, None, PAGE_SIZE, N_HEADS, HEAD_DIM),
                                lambda i, f, pt: (0, pt[seq_of(i, f), first_page(i, f) - j], 0, 0, 0))

        in_specs += [
            pl.BlockSpec((None, N_HEADS, HEAD_DIM), lambda i, f, pt: (seq_of(i, f), 0, 0)),
            pl.BlockSpec((N_HEADS, LANES), lambda i, f, pt: (0, 0)),
            pl.BlockSpec((N_HEADS, HEAD_DIM), lambda i, f, pt: (0, 0)),
        ] + [page_spec(j) for j in range(pages_per_step)] * 2
        out_specs.append(pl.BlockSpec((None, N_HEADS, HEAD_DIM), lambda i, f, pt: (seq_of(i, f), 0, 0)))
        out_shape.append(jax.ShapeDtypeStruct((db, N_HEADS, HEAD_DIM), F32))
        scratch += [pltpu.VMEM((N_HEADS, HEAD_DIM), F32), pltpu.VMEM((N_HEADS, LANES), F32)]
        args += [q, bias_tile, sb_norm_w] + [cache_k] * pages_per_step + [cache_v] * pages_per_step
        prefetch = [page_table]
    outs = pl.pallas_call(
        functools.partial(_ffn_kernel, pages_per_step=pages_per_step, steps_per_seq=steps_per_seq),
        grid_spec=pltpu.PrefetchScalarGridSpec(
            num_scalar_prefetch=len(prefetch), grid=(n_m, n_f),
            in_specs=in_specs, out_specs=out_specs, scratch_shapes=scratch),
        out_shape=out_shape,
        compiler_params=_cparams(2),
        name=name,
    )(*prefetch, *args)
    return outs[0] if attn is None else outs


def kernel(x_prompt, x_sample, cache_k, cache_v, state_gdn, state_conv, page_table, w_in, conv_w, a_log,
           dt_bias, gdn_norm_w, sb_norm_w, sb_bias, w_out, ln1_g, ln1_b, w_up, w_down, ln2_g, ln2_b):
    bp, sp, d = x_prompt.shape
    db = x_sample.shape[0]
    assert w_in.shape[0] == 1 and x_sample.shape[1] == 1

    w = w_in[0]
    o_z = CONV_DIM
    o_b = o_z + HEADS_WIDTH
    o_a = o_b + N_HEADS
    o_q = o_a + N_HEADS
    o_k = o_q + HEADS_WIDTH
    o_v = o_k + HEADS_WIDTH
    w_main = w[:, :o_b].astype(BF16)
    w_k = w[:, o_k:o_v].astype(BF16)
    w_v = w[:, o_v:].astype(BF16)
    w_qba = jnp.concatenate(
        [w[:, o_q:o_k], w[:, o_b:o_q], jnp.zeros((d, LANES - 2 * N_HEADS), w.dtype)], axis=1).astype(BF16)
    w_og = w_out[0, :HEADS_WIDTH].astype(BF16)
    w_os = w_out[0, HEADS_WIDTH:].astype(BF16)
    wu = w_up[0].astype(BF16)
    wd = w_down[0].astype(BF16)
    pad_row = lambda v: jnp.pad(v, (N_HEADS, LANES - 2 * N_HEADS)).reshape(1, LANES)
    alog_row = pad_row(a_log[0])
    dtb_row = pad_row(dt_bias[0])
    nw_row = gdn_norm_w[0].reshape(1, HEAD_DIM)
    cw = conv_w[0]
    row = lambda v: v.reshape(1, d)
    bias_tile = jnp.broadcast_to(sb_bias[0][:, None], (N_HEADS, LANES))

    xp = x_prompt.reshape(bp * sp, d)
    p1 = _matmul(xp, w_main, tm=512, tn=1024, name="proj_main_p")
    sqp, bap = _matmul_split(xp, w_qba, tm=512, name="proj_qba_p",
                             outs=[(HEADS_WIDTH, BF16, SB_QSCALE), (LANES, F32, None)])
    skp, skp_b = _matmul(xp, w_k, tm=512, tn=1024, name="proj_k_p", out_dtypes=(F32, BF16))
    svp, svp_b = _matmul(xp, w_v, tm=512, tn=1024, name="proj_v_p", out_dtypes=(F32, BF16))
    og_p, gdn_p = _gdn_prompt(p1, bap, cw, alog_row, dtb_row, nw_row, batch=bp, seq=sp, blk=256)
    os_p = _sb_prompt(sqp, skp_b, svp_b, sb_bias[0], sb_norm_w[0], batch=bp, seq=sp, tq=512)
    h_p = _out_ln(og_p, os_p, xp, w_og, w_os, row(ln1_g[0]), row(ln1_b[0]), tm=512, name="out_ln_p")
    conv_p = p1.reshape(bp, sp, -1)[:, sp - (CONV_W - 1):, :CONV_DIM]

    xs = x_sample.reshape(db, d)
    s1 = _matmul(xs, w_main, tm=db, tn=1024, name="proj_main_s")
    sqs, bas = _matmul_split(xs, w_qba, tm=db, name="proj_qba_s",
                             outs=[(HEADS_WIDTH, F32, SB_QSCALE), (LANES, F32, None)])
    sks = _matmul(xs, w_k, tm=db, tn=1024, name="proj_k_s")
    svs = _matmul(xs, w_v, tm=db, tn=1024, name="proj_v_s")
    og_s, gdn_s, conv_s = _gdn_sample(s1[:, :CONV_DIM], s1[:, CONV_DIM:], bas,
                                      state_conv[0], state_gdn[0], cw, alog_row, dtb_row, nw_row)
    y_p, os_s = _ffn(h_p, wu, wd, row(ln2_g[0]), row(ln2_b[0]), tm=512, tf=512, name="ffn_p",
                     attn=(sqs.reshape(db, N_HEADS, HEAD_DIM), cache_k, cache_v, page_table, bias_tile,
                           sb_norm_w[0], 8))
    h_s = _out_ln(og_s.reshape(db, HEADS_WIDTH), os_s.reshape(db, HEADS_WIDTH), xs, w_og, w_os,
                  row(ln1_g[0]), row(ln1_b[0]), tm=db, name="out_ln_s")
    y_s = _ffn(h_s, wu, wd, row(ln2_g[0]), row(ln2_b[0]), tm=db, tf=512, name="ffn_s")

    n_pg = sp // PAGE_SIZE
    return (
        y_p.reshape(bp, sp, d),
        y_s.reshape(db, 1, d),
        skp.reshape(1, bp, n_pg, PAGE_SIZE, N_HEADS, HEAD_DIM),
        svp.reshape(1, bp, n_pg, PAGE_SIZE, N_HEADS, HEAD_DIM),
        gdn_p[None],
        conv_p[None],
        sks.reshape(1, db, 1, N_HEADS, HEAD_DIM),
        svs.reshape(1, db, 1, N_HEADS, HEAD_DIM),
        gdn_s[None],
        conv_s[None],
    )
```

```python
import functools

import jax
import jax.numpy as jnp
from jax import lax
from jax.experimental import pallas as pl
from jax.experimental.pallas import tpu as pltpu

F32 = jnp.float32
BF16 = jnp.bfloat16

LANES = 128
HEAD_DIM = 128
N_HEADS = 8
HEADS_WIDTH = N_HEADS * HEAD_DIM
CONV_W = 4
CONV_DIM = 3 * HEADS_WIDTH
GDN_CHUNK = 64
PAIR = 2 * GDN_CHUNK
PAGE_SIZE = 128
LN_EPS = 1e-5
RMS_EPS = 1e-6
DN_ALPHA = 2.0 ** 0.25
VMEM_LIMIT = 52 * 1024 * 1024
FFN_TILE = 512
HEAD_GROUP = 8
LOG2E = 1.4426950408889634
SB_QSCALE = HEAD_DIM ** -0.5 * LOG2E

_HI = lax.Precision.HIGHEST


def _cparams(n_axes):
    return pltpu.CompilerParams(dimension_semantics=("arbitrary",) * n_axes,
                                vmem_limit_bytes=VMEM_LIMIT)


def _softplus(x):
    return jnp.maximum(x, 0.0) + jnp.log1p(jnp.exp(-jnp.abs(x)))


def _silu(x):
    h = 0.5 * x
    return h + h * jnp.tanh(h)


def _neg_softplus2(z2):
    return jnp.minimum(-z2, 0.0) - jnp.log2(1.0 + jnp.exp2(-jnp.abs(z2)))


def _dot(a, b):
    return jnp.dot(a, b, preferred_element_type=F32)


def _dot_nt(a, b):
    return lax.dot_general(a, b, (((1,), (1,)), ((), ())), preferred_element_type=F32)


def _dot_tn(a, b):
    return lax.dot_general(a, b, (((0,), (0,)), ((), ())), preferred_element_type=F32)


def _dot_hi(a, b):
    return jnp.dot(a, b, preferred_element_type=F32, precision=_HI)


def _layernorm(pre, g, b):
    mu = jnp.mean(pre, axis=-1, keepdims=True)
    d = pre - mu
    var = jnp.mean(d * d, axis=-1, keepdims=True)
    return d * lax.rsqrt(var + LN_EPS) * g + b


def _mm_kernel(x_ref, w_ref, *o_refs):
    acc = _dot(x_ref[...].astype(BF16), w_ref[...])
    for o_ref in o_refs:
        o_ref[...] = acc.astype(o_ref.dtype)


def _matmul(x, w, *, tm, tn, name, out_dtypes=(F32,)):
    m, k = x.shape
    n = w.shape[1]
    outs = pl.pallas_call(
        _mm_kernel,
        grid=(n // tn, m // tm),
        in_specs=[pl.BlockSpec((tm, k), lambda j, i: (i, 0)),
                  pl.BlockSpec((k, tn), lambda j, i: (0, j))],
        out_specs=[pl.BlockSpec((tm, tn), lambda j, i: (i, j)) for _ in out_dtypes],
        out_shape=[jax.ShapeDtypeStruct((m, n), dt) for dt in out_dtypes],
        compiler_params=_cparams(2),
        name=name,
    )(x, w)
    return outs if len(outs) > 1 else outs[0]


def _mm_split_kernel(x_ref, w_ref, *o_refs, scales):
    acc = _dot(x_ref[...].astype(BF16), w_ref[...])
    col = 0
    for o_ref, scale in zip(o_refs, scales):
        part = acc[:, col:col + o_ref.shape[-1]]
        col += o_ref.shape[-1]
        o_ref[...] = (part if scale is None else part * scale).astype(o_ref.dtype)


def _matmul_split(x, w, *, tm, name, outs):
    m, k = x.shape
    n = w.shape[1]
    assert n == sum(width for width, _, _ in outs)
    return pl.pallas_call(
        functools.partial(_mm_split_kernel, scales=tuple(s for _, _, s in outs)),
        grid=(m // tm,),
        in_specs=[pl.BlockSpec((tm, k), lambda i: (i, 0)),
                  pl.BlockSpec((k, n), lambda i: (0, 0))],
        out_specs=[pl.BlockSpec((tm, width), lambda i: (i, 0)) for width, _, _ in outs],
        out_shape=[jax.ShapeDtypeStruct((m, width), dt) for width, dt, _ in outs],
        compiler_params=_cparams(1),
        name=name,
    )(x, w)


def _gdn_prompt_kernel(conv_ref, z_ref, ba_ref, cw_ref, alog_ref, dtb_ref, nw_ref,
                       o_ref, st_ref,
                       ext_ref, q_s, k_s, v_s, beta_s, g_s, *, blk):
    j = pl.program_id(1)

    @pl.when(j == 0)
    def _():
        ext_ref[0:8, :] = jnp.zeros((8, CONV_DIM), F32)
        st_ref[...] = jnp.zeros_like(st_ref)

    @pl.when(j > 0)
    def _():
        ext_ref[0:8, :] = ext_ref[blk:blk + 8, :]

    ext_ref[8:8 + blk, :] = conv_ref[...]

    for c in range(CONV_DIM // LANES):
        cs = slice(c * LANES, (c + 1) * LANES)
        e = ext_ref[:, cs]
        acc = pltpu.roll(e, CONV_W - 1, axis=0)[8:] * cw_ref[0:1, cs]
        for k in range(1, CONV_W - 1):
            acc = acc + pltpu.roll(e, CONV_W - 1 - k, axis=0)[8:] * cw_ref[k:k + 1, cs]
        acc = acc + e[8:] * cw_ref[CONV_W - 1:CONV_W, cs]
        cu = _silu(acc)
        if c < 2 * N_HEADS:
            ss = jnp.sum(cu * cu, axis=-1, keepdims=True)
            cu = cu * lax.rsqrt(ss + RMS_EPS)
        if c < N_HEADS:
            q_s[:, cs] = cu * (HEAD_DIM ** -0.5)
        elif c < 2 * N_HEADS:
            k_s[:, (c - N_HEADS) * LANES:(c - N_HEADS + 1) * LANES] = cu
        else:
            v_s[:, (c - 2 * N_HEADS) * LANES:(c - 2 * N_HEADS + 1) * LANES] = cu

    ba = ba_ref[...]
    beta_s[...] = jax.nn.sigmoid(ba)
    g_s[...] = -jnp.exp(alog_ref[...]) * _softplus(ba + dtb_ref[...])

    row = lax.broadcasted_iota(jnp.int32, (PAIR, PAIR), 0)
    col = lax.broadcasted_iota(jnp.int32, (PAIR, PAIR), 1)
    same = (row // GDN_CHUNK) == (col // GDN_CHUNK)
    tri_incl = same & (row >= col)
    tri_strict = same & (row > col)
    cum_l = tri_incl.astype(F32)
    nw = nw_ref[...]
    half_rows = lax.broadcasted_iota(jnp.int32, (PAIR, 1), 0) // GDN_CHUNK
    halves = [slice(half * GDN_CHUNK, (half + 1) * GDN_CHUNK) for half in range(2)]

    def pair_body(p, carry):
        r0 = pl.multiple_of(p * PAIR, PAIR)
        rows = pl.ds(r0, PAIR)
        g_p = g_s[rows, :]
        beta_p = beta_s[rows, :]
        gc_all = _dot_hi(cum_l, g_p)
        gct_all = jnp.transpose(gc_all)
        for h0 in range(0, N_HEADS, HEAD_GROUP):
            heads = range(h0, h0 + HEAD_GROUP)
            hsl = {h: slice(h * HEAD_DIM, (h + 1) * HEAD_DIM) for h in heads}
            kh = {h: k_s[rows, hsl[h]] for h in heads}
            gcol = {h: gc_all[:, N_HEADS + h:N_HEADS + h + 1] for h in heads}
            gexp = {h: jnp.exp(gcol[h]) for h in heads}
            khb = {h: kh[h].astype(BF16) for h in heads}
            kb = {h: kh[h] * beta_p[:, h:h + 1] for h in heads}
            decay, xp, tm = {}, {}, {}
            for h in heads:
                grow = gct_all[N_HEADS + h:N_HEADS + h + 1, :]
                decay[h] = jnp.exp(jnp.where(tri_incl, gcol[h] - grow, -jnp.inf))
                lower = jnp.where(tri_strict, _dot_nt(kb[h].astype(BF16), khb[h]) * decay[h], 0.0)
                xp[h] = -lower
                tm[h] = xp[h]
            for _ in range(5):
                for h in heads:
                    xb = xp[h].astype(BF16)
                    xp[h] = _dot(xb, xb)
                for h in heads:
                    tm[h] = tm[h] + xp[h] + _dot(tm[h].astype(BF16), xp[h].astype(BF16))
            u, kcb, qk, qg = {}, {}, {}, {}
            for h in heads:
                rhs = jnp.concatenate([v_s[rows, hsl[h]] * beta_p[:, h:h + 1], kb[h] * gexp[h]], axis=1)
                uk = rhs + _dot(tm[h].astype(BF16), rhs.astype(BF16))
                u[h] = uk[:, :HEAD_DIM]
                kcb[h] = uk[:, HEAD_DIM:].astype(BF16)
                qh = q_s[rows, hsl[h]]
                qk[h] = jnp.where(tri_incl, _dot_nt(qh.astype(BF16), khb[h]) * decay[h], 0.0).astype(BF16)
                qg[h] = (qh * gexp[h]).astype(BF16)
            st = {h: st_ref[h] for h in heads}
            o_halves = {h: [] for h in heads}
            for half in range(2):
                hr = halves[half]
                in_half = half_rows == half
                ks_qs = {h: _dot(jnp.concatenate([kcb[h][hr], qg[h][hr]], axis=0), st[h].astype(BF16))
                         for h in heads}
                for h in heads:
                    v_new = u[h][hr] - ks_qs[h][:GDN_CHUNK]
                    v_full = jnp.where(in_half, jnp.concatenate([v_new, v_new], axis=0), 0.0).astype(BF16)
                    o_halves[h].append(ks_qs[h][GDN_CHUNK:] + _dot(qk[h][hr], v_full))
                    g_last = gcol[h][(half + 1) * GDN_CHUNK - 1:(half + 1) * GDN_CHUNK, :]
                    kdec = jnp.where(in_half, kh[h] * jnp.exp(g_last - gcol[h]), 0.0).astype(BF16)
                    st[h] = st[h] * jnp.exp(g_last) + _dot_tn(kdec, v_full)
            for h in heads:
                st_ref[h] = st[h]
                o = jnp.concatenate(o_halves[h], axis=0)
                o = o * lax.rsqrt(jnp.mean(o * o, axis=-1, keepdims=True) + RMS_EPS) * nw
                zh = z_ref[rows, hsl[h]]
                o_ref[rows, hsl[h]] = (o * _silu(zh)).astype(o_ref.dtype)
        return carry

    lax.fori_loop(0, blk // PAIR, pair_body, 0, unroll=True)


def _gdn_prompt(p1, ba, conv_w, alog_row, dtb_row, nw_row, *, batch, seq, blk):
    nblk = seq // blk
    z_col = CONV_DIM // HEADS_WIDTH
    kern = functools.partial(_gdn_prompt_kernel, blk=blk)
    return pl.pallas_call(
        kern,
        grid=(batch, nblk),
        in_specs=[
            pl.BlockSpec((blk, CONV_DIM), lambda b, j: (b * nblk + j, 0)),
            pl.BlockSpec((blk, HEADS_WIDTH), lambda b, j: (b * nblk + j, z_col)),
            pl.BlockSpec((blk, LANES), lambda b, j: (b * nblk + j, 0)),
            pl.BlockSpec((CONV_W, CONV_DIM), lambda b, j: (0, 0)),
            pl.BlockSpec((1, LANES), lambda b, j: (0, 0)),
            pl.BlockSpec((1, LANES), lambda b, j: (0, 0)),
            pl.BlockSpec((1, HEAD_DIM), lambda b, j: (0, 0)),
        ],
        out_specs=[
            pl.BlockSpec((blk, HEADS_WIDTH), lambda b, j: (b * nblk + j, 0)),
            pl.BlockSpec((None, N_HEADS, HEAD_DIM, HEAD_DIM), lambda b, j: (b, 0, 0, 0)),
        ],
        out_shape=[
            jax.ShapeDtypeStruct((batch * seq, HEADS_WIDTH), BF16),
            jax.ShapeDtypeStruct((batch, N_HEADS, HEAD_DIM, HEAD_DIM), F32),
        ],
        scratch_shapes=[
            pltpu.VMEM((blk + 8, CONV_DIM), F32),
            pltpu.VMEM((blk, HEADS_WIDTH), F32),
            pltpu.VMEM((blk, HEADS_WIDTH), F32),
            pltpu.VMEM((blk, HEADS_WIDTH), F32),
            pltpu.VMEM((blk, LANES), F32),
            pltpu.VMEM((blk, LANES), F32),
        ],
        compiler_params=_cparams(2),
        name="gdn_prompt",
    )(p1, p1, ba, conv_w, alog_row, dtb_row, nw_row)


def _gdn_sample_kernel(u_ref, z_ref, ba_ref, c0_ref, s0_ref, cw_ref, alog_ref, dtb_ref, nw_ref,
                       o_ref, st_ref, cn_ref):
    u = u_ref[0]
    c0 = c0_ref[...]
    acc = c0[0:1] * cw_ref[0:1, :]
    acc = acc + c0[1:2] * cw_ref[1:2, :]
    acc = acc + c0[2:3] * cw_ref[2:3, :]
    acc = acc + u * cw_ref[3:4, :]
    cu = _silu(acc)
    cn_ref[0:1, :] = c0[1:2]
    cn_ref[1:2, :] = c0[2:3]
    cn_ref[2:3, :] = u

    ba = ba_ref[0]
    beta_all = jax.nn.sigmoid(ba)
    g_all = -jnp.exp(alog_ref[...]) * _softplus(ba + dtb_ref[...])
    z = z_ref[0]
    nw = nw_ref[...]
    first_row = lax.broadcasted_iota(jnp.int32, (8, HEAD_DIM), 0) == 0
    for h in range(N_HEADS):
        hs = slice(h * HEAD_DIM, (h + 1) * HEAD_DIM)
        q = cu[:, hs]
        k = cu[:, HEADS_WIDTH + h * HEAD_DIM:HEADS_WIDTH + (h + 1) * HEAD_DIM]
        v = cu[:, 2 * HEADS_WIDTH + h * HEAD_DIM:2 * HEADS_WIDTH + (h + 1) * HEAD_DIM]
        q = q * lax.rsqrt(jnp.sum(q * q, axis=-1, keepdims=True) + RMS_EPS) * (HEAD_DIM ** -0.5)
        k = k * lax.rsqrt(jnp.sum(k * k, axis=-1, keepdims=True) + RMS_EPS)
        beta = beta_all[:, h:h + 1]
        g = g_all[:, N_HEADS + h:N_HEADS + h + 1]
        eg = jnp.exp(g)
        st = s0_ref[h]
        stb = st.astype(BF16)
        kcum = jnp.broadcast_to(k * beta * eg, (8, HEAD_DIM)).astype(BF16)
        qg = jnp.broadcast_to(q * eg, (8, HEAD_DIM)).astype(BF16)
        v_new = v * beta - _dot(kcum, stb)[0:1]
        qk = jnp.sum(q.astype(BF16).astype(F32) * k.astype(BF16).astype(F32), axis=-1, keepdims=True)
        o = _dot(qg, stb)[0:1] + qk * v_new.astype(BF16).astype(F32)
        k8 = jnp.where(first_row, jnp.broadcast_to(k, (8, HEAD_DIM)), 0.0).astype(BF16)
        v8 = jnp.broadcast_to(v_new, (8, HEAD_DIM)).astype(BF16)
        st_ref[h] = st * eg + _dot_tn(k8, v8)
        o = o * lax.rsqrt(jnp.mean(o * o, axis=-1, keepdims=True) + RMS_EPS) * nw
        zh = z[:, hs]
        o_ref[0, :, hs] = (o * _silu(zh)).astype(o_ref.dtype)


def _gdn_sample(conv_in, z, ba, conv0, s0, conv_w, alog_row, dtb_row, nw_row):
    db = conv_in.shape[0]
    row3 = lambda a: a.reshape(db, 1, a.shape[-1])
    full = lambda shape: pl.BlockSpec(shape, lambda b: (0,) * len(shape))
    return pl.pallas_call(
        _gdn_sample_kernel,
        grid=(db,),
        in_specs=[
            pl.BlockSpec((1, 1, CONV_DIM), lambda b: (b, 0, 0)),
            pl.BlockSpec((1, 1, HEADS_WIDTH), lambda b: (b, 0, 0)),
            pl.BlockSpec((1, 1, LANES), lambda b: (b, 0, 0)),
            pl.BlockSpec((None, CONV_W - 1, CONV_DIM), lambda b: (b, 0, 0)),
            pl.BlockSpec((None, N_HEADS, HEAD_DIM, HEAD_DIM), lambda b: (b, 0, 0, 0)),
            full((CONV_W, CONV_DIM)),
            full((1, LANES)),
            full((1, LANES)),
            full((1, HEAD_DIM)),
        ],
        out_specs=[
            pl.BlockSpec((1, 1, HEADS_WIDTH), lambda b: (b, 0, 0)),
            pl.BlockSpec((None, N_HEADS, HEAD_DIM, HEAD_DIM), lambda b: (b, 0, 0, 0)),
            pl.BlockSpec((None, CONV_W - 1, CONV_DIM), lambda b: (b, 0, 0)),
        ],
        out_shape=[
            jax.ShapeDtypeStruct((db, 1, HEADS_WIDTH), F32),
            jax.ShapeDtypeStruct((db, N_HEADS, HEAD_DIM, HEAD_DIM), F32),
            jax.ShapeDtypeStruct((db, CONV_W - 1, CONV_DIM), F32),
        ],
        compiler_params=_cparams(1),
        name="gdn_sample",
    )(row3(conv_in), row3(z), row3(ba), conv0, s0, conv_w, alog_row, dtb_row, nw_row)


def _suffix_matrix():
    j = lax.broadcasted_iota(jnp.int32, (LANES, 2 * LANES), 0)
    s = lax.broadcasted_iota(jnp.int32, (LANES, 2 * LANES), 1)
    return ((j >= s) | (s >= LANES)).astype(BF16)


def _suffix_sums(m, uj):
    cs = _dot(m.astype(BF16), uj)
    return cs[:, :LANES], cs[:, LANES:]


def _sb_prompt_kernel(bias_ref, q_ref, k_ref, v_ref, nw_ref, o_ref, acc_ref, r_ref, *, tq):
    h = pl.program_id(1)
    i = pl.program_id(2)
    bias2 = bias_ref[h] * LOG2E
    uj = _suffix_matrix()
    nsub = tq // LANES

    diag_keys = [pl.ds(pl.multiple_of(i * tq + c * LANES, LANES), LANES) for c in range(nsub)]
    z, valid, sums = {}, {}, {}
    for c in reversed(range(nsub)):
        nrow = tq - c * LANES
        z[c] = _dot_nt(q_ref[c * LANES:tq, :], k_ref[diag_keys[c], :]) + bias2
        valid[c] = (lax.broadcasted_iota(jnp.int32, (nrow, LANES), 1)
                    < lax.broadcasted_iota(jnp.int32, (nrow, LANES), 0))
    for c in reversed(range(nsub)):
        sums[c] = _suffix_sums(jnp.where(valid[c], _neg_softplus2(z[c]), 0.0), uj)
    acc = jnp.zeros((tq, HEAD_DIM), F32)
    run = jnp.zeros((tq, LANES), F32)
    for c in reversed(range(nsub)):
        incl, tot = sums[c]
        pad = jnp.zeros((c * LANES, LANES), F32)
        a = jnp.where(valid[c], jnp.exp2(z[c] + incl + run[c * LANES:]), 0.0).astype(BF16)
        upd = _dot(a, v_ref[diag_keys[c], :])
        acc = acc + (jnp.concatenate([pad, upd], axis=0) if c else upd)
        run = run + (jnp.concatenate([pad, tot], axis=0) if c else tot)
    acc_ref[...] = acc
    r_ref[...] = run

    def body(n, carry):
        ks = pl.ds(pl.multiple_of((i - 1 - n) * tq, tq), tq)
        z = _dot_nt(q_ref[...], k_ref[ks, :]) + bias2
        m = _neg_softplus2(z)
        run = r_ref[...]
        parts = [None] * nsub
        for c in reversed(range(nsub)):
            cs = slice(c * LANES, (c + 1) * LANES)
            incl, tot = _suffix_sums(m[:, cs], uj)
            parts[c] = jnp.exp2(z[:, cs] + incl + run).astype(BF16)
            run = run + tot
        r_ref[...] = run
        acc_ref[...] += _dot(jnp.concatenate(parts, axis=1), v_ref[ks, :])
        return carry

    lax.fori_loop(0, i, body, 0)

    o = acc_ref[...]
    o = o * lax.rsqrt(jnp.mean(o * o, axis=-1, keepdims=True) + RMS_EPS) * nw_ref[pl.ds(h, 1), :]
    o_ref[...] = o.astype(o_ref.dtype)


def _sb_prompt(q, k, v, sb_bias, sb_norm_w, *, batch, seq, tq):
    nq = seq // tq
    kern = functools.partial(_sb_prompt_kernel, tq=tq)
    grid_spec = pltpu.PrefetchScalarGridSpec(
        num_scalar_prefetch=1,
        grid=(batch, N_HEADS, nq),
        in_specs=[
            pl.BlockSpec((tq, HEAD_DIM), lambda b, h, i, bias: (b * nq + i, h)),
            pl.BlockSpec((seq, HEAD_DIM), lambda b, h, i, bias: (b, h)),
            pl.BlockSpec((seq, HEAD_DIM), lambda b, h, i, bias: (b, h)),
            pl.BlockSpec((N_HEADS, HEAD_DIM), lambda b, h, i, bias: (0, 0)),
        ],
        out_specs=pl.BlockSpec((tq, HEAD_DIM), lambda b, h, i, bias: (b * nq + i, h)),
        scratch_shapes=[pltpu.VMEM((tq, HEAD_DIM), F32), pltpu.VMEM((tq, LANES), F32)],
    )
    return pl.pallas_call(
        kern,
        grid_spec=grid_spec,
        out_shape=jax.ShapeDtypeStruct((batch * seq, HEADS_WIDTH), BF16),
        compiler_params=_cparams(3),
        name="sb_prompt",
    )(sb_bias, q, k, v, sb_norm_w)


_PAGE_ROWS = PAGE_SIZE * N_HEADS
_PAGE_CHUNKS = _PAGE_ROWS // LANES


def _own_head_lanes():
    sub = lax.broadcasted_iota(jnp.int32, (N_HEADS, LANES), 0)
    lane = lax.broadcasted_iota(jnp.int32, (N_HEADS, LANES), 1)
    return (lane % N_HEADS) == sub


def _sb_page_scores(q_ref, bias_ref, k_refs):
    qb = q_ref[...].astype(BF16)
    bias2 = bias_ref[...] * LOG2E
    zc = []
    for k_ref in k_refs:
        z = _dot_nt(qb, k_ref[...].reshape(_PAGE_ROWS, HEAD_DIM).astype(BF16))
        zc += [z[:, c * LANES:(c + 1) * LANES] + bias2 for c in range(_PAGE_CHUNKS)]
    return zc


def _sb_page_suffix(zc):
    own = _own_head_lanes()
    wl = lax.broadcasted_iota(jnp.int32, (LANES, 2 * LANES), 0)
    wc = lax.broadcasted_iota(jnp.int32, (LANES, 2 * LANES), 1)
    wsuf = ((wl // N_HEADS >= wc // N_HEADS) | (wc >= LANES)).astype(BF16)
    m = jnp.concatenate([jnp.where(own, _neg_softplus2(zz), 0.0) for zz in zc], axis=0)
    m_hi = m.astype(BF16)
    m_lo = (m - m_hi.astype(F32)).astype(BF16)
    return _dot(m_hi, wsuf) + _dot(m_lo, wsuf)


def _sb_page_fold(zc, cs, v_refs, acc_ref, r_ref):
    own = _own_head_lanes()
    run = r_ref[...]
    acc = acc_ref[...]
    for j, v_ref in enumerate(v_refs):
        parts = [None] * _PAGE_CHUNKS
        for c in reversed(range(_PAGE_CHUNKS)):
            n = j * _PAGE_CHUNKS + c
            rs = slice(n * N_HEADS, (n + 1) * N_HEADS)
            e = zc[n] + cs[rs, :LANES] + run
            parts[c] = jnp.where(own, jnp.exp2(e), 0.0).astype(BF16)
            run = run + cs[rs, LANES:]
        vf = v_ref[...].reshape(_PAGE_ROWS, HEAD_DIM).astype(BF16)
        acc = acc + _dot(jnp.concatenate(parts, axis=1), vf)
    r_ref[...] = run
    acc_ref[...] = acc


def _out_ln_kernel(og_ref, os_ref, x_ref, wg_ref, ws_ref, g_ref, b_ref, o_ref):
    mixed = _dot(og_ref[...].astype(BF16), wg_ref[...]) + _dot(os_ref[...].astype(BF16), ws_ref[...])
    o_ref[...] = _layernorm(DN_ALPHA * x_ref[...] + mixed, g_ref[...], b_ref[...])


def _out_ln(og, osb, x, wg, ws, g, b, *, tm, name):
    m, d = x.shape
    return pl.pallas_call(
        _out_ln_kernel,
        grid=(m // tm,),
        in_specs=[
            pl.BlockSpec((tm, HEADS_WIDTH), lambda i: (i, 0)),
            pl.BlockSpec((tm, HEADS_WIDTH), lambda i: (i, 0)),
            pl.BlockSpec((tm, d), lambda i: (i, 0)),
            pl.BlockSpec((HEADS_WIDTH, d), lambda i: (0, 0)),
            pl.BlockSpec((HEADS_WIDTH, d), lambda i: (0, 0)),
            pl.BlockSpec((1, d), lambda i: (0, 0)),
            pl.BlockSpec((1, d), lambda i: (0, 0)),
        ],
        out_specs=pl.BlockSpec((tm, d), lambda i: (i, 0)),
        out_shape=jax.ShapeDtypeStruct((m, d), F32),
        compiler_params=_cparams(1),
        name=name,
    )(og, osb, x, wg, ws, g, b)


def _ffn_kernel(*refs, pages_per_step, steps_per_seq):
    if pages_per_step:
        _, h_ref, wu_ref, wd_ref, g_ref, b_ref, q_ref, bias_ref, nw_ref = refs[:9]
        k_refs = refs[9:9 + pages_per_step]
        v_refs = refs[9 + pages_per_step:9 + 2 * pages_per_step]
        o_ref, os_ref, hb_ref, acc_ref, r_ref = refs[9 + 2 * pages_per_step:]
    else:
        h_ref, wu_ref, wd_ref, g_ref, b_ref, o_ref, hb_ref = refs
    f = pl.program_id(1)
    n_f = pl.num_programs(1)

    @pl.when(f == 0)
    def _():
        hb_ref[...] = h_ref[...].astype(BF16)
        o_ref[...] = jnp.zeros_like(o_ref)

    if pages_per_step:
        s = (pl.program_id(0) * n_f + f) % steps_per_seq

        @pl.when(s == 0)
        def _():
            acc_ref[...] = jnp.zeros_like(acc_ref)
            r_ref[...] = jnp.zeros_like(r_ref)

    if pages_per_step:
        zc = _sb_page_scores(q_ref, bias_ref, k_refs)
    a = jnp.maximum(_dot(hb_ref[...], wu_ref[...]), 0.0)
    if pages_per_step:
        cs = _sb_page_suffix(zc)
    o_ref[...] += _dot((a * a).astype(BF16), wd_ref[...])
    if pages_per_step:
        _sb_page_fold(zc, cs, v_refs, acc_ref, r_ref)

        @pl.when(s == steps_per_seq - 1)
        def _():
            o = acc_ref[...]
            o = o * lax.rsqrt(jnp.mean(o * o, axis=-1, keepdims=True) + RMS_EPS) * nw_ref[...]
            os_ref[...] = o.astype(os_ref.dtype)

    @pl.when(f == n_f - 1)
    def _():
        o_ref[...] = _layernorm(DN_ALPHA * h_ref[...] + o_ref[...], g_ref[...], b_ref[...])


def _ffn(h, wu, wd, g, b, *, tm, name, attn=None):
    m, d = h.shape
    n_f, _, tf = wu.shape
    n_m = m // tm
    in_specs = [
        pl.BlockSpec((tm, d), lambda i, f, *_: (i, 0)),
        pl.BlockSpec((None, d, tf), lambda i, f, *_: (f, 0, 0)),
        pl.BlockSpec((tf, d), lambda i, f, *_: (f, 0)),
        pl.BlockSpec((1, d), lambda i, f, *_: (0, 0)),
        pl.BlockSpec((1, d), lambda i, f, *_: (0, 0)),
    ]
    out_specs = [pl.BlockSpec((tm, d), lambda i, f, *_: (i, 0))]
    out_shape = [jax.ShapeDtypeStruct((m, d), F32)]
    scratch = [pltpu.VMEM((tm, d), BF16)]
    args = [h, wu, wd, g, b]
    if attn is None:
        pages_per_step = steps_per_seq = 0
        prefetch = []
    else:
        q, cache_k, cache_v, page_table, bias_tile, sb_norm_w, pages_per_step = attn
        db, n_pages = page_table.shape
        steps_per_seq = n_pages // pages_per_step
        assert n_m * n_f == db * steps_per_seq, "attention steps must tile the FFN grid exactly"
        seq_of = lambda i, f: (i * n_f + f) // steps_per_seq
        first_page = lambda i, f: n_pages - 1 - ((i * n_f + f) % steps_per_seq) * pages_per_step

        def page_spec(j):
            return pl.BlockSpec((None, None, PAGE_SIZE, N_HEADS, HEAD_DIM),
                                lambda i, f, pt: (0, pt[seq_of(i, f), first_page(i, f) - j], 0, 0, 0))

        in_specs += [
            pl.BlockSpec((None, N_HEADS, HEAD_DIM), lambda i, f, pt: (seq_of(i, f), 0, 0)),
            pl.BlockSpec((N_HEADS, LANES), lambda i, f, pt: (0, 0)),
            pl.BlockSpec((N_HEADS, HEAD_DIM), lambda i, f, pt: (0, 0)),
        ] + [page_spec(j) for j in range(pages_per_step)] * 2
        out_specs.append(pl.BlockSpec((None, N_HEADS, HEAD_DIM), lambda i, f, pt: (seq_of(i, f), 0, 0)))
        out_shape.append(jax.ShapeDtypeStruct((db, N_HEADS, HEAD_DIM), F32))
        scratch += [pltpu.VMEM((N_HEADS, HEAD_DIM), F32), pltpu.VMEM((N_HEADS, LANES), F32)]
        args += [q, bias_tile, sb_norm_w] + [cache_k] * pages_per_step + [cache_v] * pages_per_step
        prefetch = [page_table]
    outs = pl.pallas_call(
        functools.partial(_ffn_kernel, pages_per_step=pages_per_step, steps_per_seq=steps_per_seq),
        grid_spec=pltpu.PrefetchScalarGridSpec(
            num_scalar_prefetch=len(prefetch), grid=(n_m, n_f),
            in_specs=in_specs, out_specs=out_specs, scratch_shapes=scratch),
        out_shape=out_shape,
        compiler_params=_cparams(2),
        name=name,
    )(*prefetch, *args)
    return outs[0] if attn is None else outs


def kernel(x_prompt, x_sample, cache_k, cache_v, state_gdn, state_conv, page_table, w_in, conv_w, a_log,
           dt_bias, gdn_norm_w, sb_norm_w, sb_bias, w_out, ln1_g, ln1_b, w_up, w_down, ln2_g, ln2_b):
    bp, sp, d = x_prompt.shape
    db = x_sample.shape[0]
    assert w_in.shape[0] == 1 and x_sample.shape[1] == 1

    w = w_in[0]
    o_z = CONV_DIM
    o_b = o_z + HEADS_WIDTH
    o_a = o_b + N_HEADS
    o_q = o_a + N_HEADS
    o_k = o_q + HEADS_WIDTH
    o_v = o_k + HEADS_WIDTH
    w_main = w[:, :o_b].astype(BF16)
    w_k = w[:, o_k:o_v].astype(BF16)
    w_v = w[:, o_v:].astype(BF16)
    w_qba = jnp.concatenate(
        [w[:, o_q:o_k], w[:, o_b:o_q], jnp.zeros((d, LANES - 2 * N_HEADS), w.dtype)], axis=1).astype(BF16)
    w_og = w_out[0, :HEADS_WIDTH].astype(BF16)
    w_os = w_out[0, HEADS_WIDTH:].astype(BF16)
    d_ff = w_up.shape[-1]
    wu = w_up[0].reshape(d, d_ff // FFN_TILE, FFN_TILE).transpose(1, 0, 2).astype(BF16)
    wd = w_down[0].astype(BF16)
    pad_row = lambda v: jnp.pad(v, (N_HEADS, LANES - 2 * N_HEADS)).reshape(1, LANES)
    alog_row = pad_row(a_log[0])
    dtb_row = pad_row(dt_bias[0])
    nw_row = gdn_norm_w[0].reshape(1, HEAD_DIM)
    cw = conv_w[0]
    row = lambda v: v.reshape(1, d)
    bias_tile = jnp.broadcast_to(sb_bias[0][:, None], (N_HEADS, LANES))

    xp = x_prompt.reshape(bp * sp, d)
    p1 = _matmul(xp, w_main, tm=512, tn=1024, name="proj_main_p")
    sqp, bap = _matmul_split(xp, w_qba, tm=512, name="proj_qba_p",
                             outs=[(HEADS_WIDTH, BF16, SB_QSCALE), (LANES, F32, None)])
    skp, skp_b = _matmul(xp, w_k, tm=512, tn=1024, name="proj_k_p", out_dtypes=(F32, BF16))
    svp, svp_b = _matmul(xp, w_v, tm=512, tn=1024, name="proj_v_p", out_dtypes=(F32, BF16))
    og_p, gdn_p = _gdn_prompt(p1, bap, cw, alog_row, dtb_row, nw_row, batch=bp, seq=sp, blk=256)
    os_p = _sb_prompt(sqp, skp_b, svp_b, sb_bias[0], sb_norm_w[0], batch=bp, seq=sp, tq=512)
    h_p = _out_ln(og_p, os_p, xp, w_og, w_os, row(ln1_g[0]), row(ln1_b[0]), tm=512, name="out_ln_p")
    conv_p = p1.reshape(bp, sp, -1)[:, sp - (CONV_W - 1):, :CONV_DIM]

    xs = x_sample.reshape(db, d)
    s1 = _matmul(xs, w_main, tm=db, tn=1024, name="proj_main_s")
    sqs, bas = _matmul_split(xs, w_qba, tm=db, name="proj_qba_s",
                             outs=[(HEADS_WIDTH, F32, SB_QSCALE), (LANES, F32, None)])
    sks = _matmul(xs, w_k, tm=db, tn=1024, name="proj_k_s")
    svs = _matmul(xs, w_v, tm=db, tn=1024, name="proj_v_s")
    og_s, gdn_s, conv_s = _gdn_sample(s1[:, :CONV_DIM], s1[:, CONV_DIM:], bas,
                                      state_conv[0], state_gdn[0], cw, alog_row, dtb_row, nw_row)
    y_p, os_s = _ffn(h_p, wu, wd, row(ln2_g[0]), row(ln2_b[0]), tm=512, name="ffn_p",
                     attn=(sqs.reshape(db, N_HEADS, HEAD_DIM), cache_k, cache_v, page_table, bias_tile,
                           sb_norm_w[0], 8))
    h_s = _out_ln(og_s.reshape(db, HEADS_WIDTH), os_s.reshape(db, HEADS_WIDTH), xs, w_og, w_os,
                  row(ln1_g[0]), row(ln1_b[0]), tm=db, name="out_ln_s")
    y_s = _ffn(h_s, wu, wd, row(ln2_g[0]), row(ln2_b[0]), tm=db, name="ffn_s")

    n_pg = sp // PAGE_SIZE
    return (
        y_p.reshape(bp, sp, d),
        y_s.reshape(db, 1, d),
        skp.reshape(1, bp, n_pg, PAGE_SIZE, N_HEADS, HEAD_DIM),
        svp.reshape(1, bp, n_pg, PAGE_SIZE, N_HEADS, HEAD_DIM),
        gdn_p[None],
        conv_p[None],
        sks.reshape(1, db, 1, N_HEADS, HEAD_DIM),
        svs.reshape(1, db, 1, N_HEADS, HEAD_DIM),
        gdn_s[None],
        conv_s[None],
    )
```

```python
import functools

import jax
import jax.numpy as jnp
from jax import lax
from jax.experimental import pallas as pl
from jax.experimental.pallas import tpu as pltpu

F32 = jnp.float32
BF16 = jnp.bfloat16

LANES = 128
HEAD_DIM = 128
N_HEADS = 8
HEADS_WIDTH = N_HEADS * HEAD_DIM
CONV_W = 4
CONV_DIM = 3 * HEADS_WIDTH
GDN_CHUNK = 64
PAIR = 2 * GDN_CHUNK
PAGE_SIZE = 128
LN_EPS = 1e-5
RMS_EPS = 1e-6
DN_ALPHA = 2.0 ** 0.25
VMEM_LIMIT = 52 * 1024 * 1024
FFN_TILE = 512
HEAD_GROUP = 8
LOG2E = 1.4426950408889634
SB_QSCALE = HEAD_DIM ** -0.5 * LOG2E

_HI = lax.Precision.HIGHEST


def _cparams(n_axes):
    return pltpu.CompilerParams(dimension_semantics=("arbitrary",) * n_axes,
                                vmem_limit_bytes=VMEM_LIMIT)


def _softplus(x):
    return jnp.maximum(x, 0.0) + jnp.log1p(jnp.exp(-jnp.abs(x)))


def _silu(x):
    h = 0.5 * x
    return h + h * jnp.tanh(h)


def _neg_softplus2(z2):
    return jnp.minimum(-z2, 0.0) - jnp.log2(1.0 + jnp.exp2(-jnp.abs(z2)))


def _dot(a, b):
    return jnp.dot(a, b, preferred_element_type=F32)


def _dot_nt(a, b):
    return lax.dot_general(a, b, (((1,), (1,)), ((), ())), preferred_element_type=F32)


def _dot_tn(a, b):
    return lax.dot_general(a, b, (((0,), (0,)), ((), ())), preferred_element_type=F32)


def _dot_hi(a, b):
    return jnp.dot(a, b, preferred_element_type=F32, precision=_HI)


def _layernorm(pre, g, b):
    mu = jnp.mean(pre, axis=-1, keepdims=True)
    d = pre - mu
    var = jnp.mean(d * d, axis=-1, keepdims=True)
    return d * lax.rsqrt(var + LN_EPS) * g + b


def _mm_kernel(x_ref, w_ref, *o_refs):
    acc = _dot(x_ref[...].astype(BF16), w_ref[...])
    for o_ref in o_refs:
        o_ref[...] = acc.astype(o_ref.dtype)


def _matmul(x, w, *, tm, tn, name, out_dtypes=(F32,)):
    m, k = x.shape
    n = w.shape[1]
    outs = pl.pallas_call(
        _mm_kernel,
        grid=(n // tn, m // tm),
        in_specs=[pl.BlockSpec((tm, k), lambda j, i: (i, 0)),
                  pl.BlockSpec((k, tn), lambda j, i: (0, j))],
        out_specs=[pl.BlockSpec((tm, tn), lambda j, i: (i, j)) for _ in out_dtypes],
        out_shape=[jax.ShapeDtypeStruct((m, n), dt) for dt in out_dtypes],
        compiler_params=_cparams(2),
        name=name,
    )(x, w)
    return outs if len(outs) > 1 else outs[0]


def _mm_split_kernel(x_ref, w_ref, *o_refs, scales):
    acc = _dot(x_ref[...].astype(BF16), w_ref[...])
    col = 0
    for o_ref, scale in zip(o_refs, scales):
        part = acc[:, col:col + o_ref.shape[-1]]
        col += o_ref.shape[-1]
        o_ref[...] = (part if scale is None else part * scale).astype(o_ref.dtype)


def _matmul_split(x, w, *, tm, name, outs):
    m, k = x.shape
    n = w.shape[1]
    assert n == sum(width for width, _, _ in outs)
    return pl.pallas_call(
        functools.partial(_mm_split_kernel, scales=tuple(s for _, _, s in outs)),
        grid=(m // tm,),
        in_specs=[pl.BlockSpec((tm, k), lambda i: (i, 0)),
                  pl.BlockSpec((k, n), lambda i: (0, 0))],
        out_specs=[pl.BlockSpec((tm, width), lambda i: (i, 0)) for width, _, _ in outs],
        out_shape=[jax.ShapeDtypeStruct((m, width), dt) for width, dt, _ in outs],
        compiler_params=_cparams(1),
        name=name,
    )(x, w)


def _gdn_prompt_kernel(conv_ref, z_ref, ba_ref, cw_ref, alog_ref, dtb_ref, nw_ref,
                       o_ref, st_ref,
                       ext_ref, q_s, k_s, v_s, beta_s, g_s, *, blk):
    j = pl.program_id(1)

    @pl.when(j == 0)
    def _():
        ext_ref[0:8, :] = jnp.zeros((8, CONV_DIM), F32)
        st_ref[...] = jnp.zeros_like(st_ref)

    @pl.when(j > 0)
    def _():
        ext_ref[0:8, :] = ext_ref[blk:blk + 8, :]

    ext_ref[8:8 + blk, :] = conv_ref[...]

    for c in range(CONV_DIM // LANES):
        cs = slice(c * LANES, (c + 1) * LANES)
        e = ext_ref[:, cs]
        acc = pltpu.roll(e, CONV_W - 1, axis=0)[8:] * cw_ref[0:1, cs]
        for k in range(1, CONV_W - 1):
            acc = acc + pltpu.roll(e, CONV_W - 1 - k, axis=0)[8:] * cw_ref[k:k + 1, cs]
        acc = acc + e[8:] * cw_ref[CONV_W - 1:CONV_W, cs]
        cu = _silu(acc)
        if c < 2 * N_HEADS:
            ss = jnp.sum(cu * cu, axis=-1, keepdims=True)
            cu = cu * lax.rsqrt(ss + RMS_EPS)
        if c < N_HEADS:
            q_s[:, cs] = cu * (HEAD_DIM ** -0.5)
        elif c < 2 * N_HEADS:
            k_s[:, (c - N_HEADS) * LANES:(c - N_HEADS + 1) * LANES] = cu
        else:
            v_s[:, (c - 2 * N_HEADS) * LANES:(c - 2 * N_HEADS + 1) * LANES] = cu

    ba = ba_ref[...]
    beta_s[...] = jax.nn.sigmoid(ba)
    g_s[...] = -jnp.exp(alog_ref[...]) * _softplus(ba + dtb_ref[...])

    row = lax.broadcasted_iota(jnp.int32, (PAIR, PAIR), 0)
    col = lax.broadcasted_iota(jnp.int32, (PAIR, PAIR), 1)
    same = (row // GDN_CHUNK) == (col // GDN_CHUNK)
    tri_incl = same & (row >= col)
    tri_strict = same & (row > col)
    cum_l = tri_incl.astype(F32)
    nw = nw_ref[...]
    half_rows = lax.broadcasted_iota(jnp.int32, (PAIR, 1), 0) // GDN_CHUNK
    halves = [slice(half * GDN_CHUNK, (half + 1) * GDN_CHUNK) for half in range(2)]

    def pair_body(p, carry):
        r0 = pl.multiple_of(p * PAIR, PAIR)
        rows = pl.ds(r0, PAIR)
        g_p = g_s[rows, :]
        beta_p = beta_s[rows, :]
        gc_all = _dot_hi(cum_l, g_p)
        gct_all = jnp.transpose(gc_all)
        for h0 in range(0, N_HEADS, HEAD_GROUP):
            heads = range(h0, h0 + HEAD_GROUP)
            hsl = {h: slice(h * HEAD_DIM, (h + 1) * HEAD_DIM) for h in heads}
            kh = {h: k_s[rows, hsl[h]] for h in heads}
            gcol = {h: gc_all[:, N_HEADS + h:N_HEADS + h + 1] for h in heads}
            gexp = {h: jnp.exp(gcol[h]) for h in heads}
            khb = {h: kh[h].astype(BF16) for h in heads}
            kb = {h: kh[h] * beta_p[:, h:h + 1] for h in heads}
            decay, xp, tm = {}, {}, {}
            for h in heads:
                grow = gct_all[N_HEADS + h:N_HEADS + h + 1, :]
                decay[h] = jnp.exp(jnp.where(tri_incl, gcol[h] - grow, -jnp.inf))
                lower = jnp.where(tri_strict, _dot_nt(kb[h].astype(BF16), khb[h]) * decay[h], 0.0)
                xp[h] = -lower
                tm[h] = xp[h]
            for _ in range(5):
                for h in heads:
                    xb = xp[h].astype(BF16)
                    xp[h] = _dot(xb, xb)
                for h in heads:
                    tm[h] = tm[h] + xp[h] + _dot(tm[h].astype(BF16), xp[h].astype(BF16))
            u, kcb, qk, qg = {}, {}, {}, {}
            for h in heads:
                rhs = jnp.concatenate([v_s[rows, hsl[h]] * beta_p[:, h:h + 1], kb[h] * gexp[h]], axis=1)
                uk = rhs + _dot(tm[h].astype(BF16), rhs.astype(BF16))
                u[h] = uk[:, :HEAD_DIM]
                kcb[h] = uk[:, HEAD_DIM:].astype(BF16)
                qh = q_s[rows, hsl[h]]
                qk[h] = jnp.where(tri_incl, _dot_nt(qh.astype(BF16), khb[h]) * decay[h], 0.0).astype(BF16)
                qg[h] = (qh * gexp[h]).astype(BF16)
            st = {h: st_ref[h] for h in heads}
            o_halves = {h: [] for h in heads}
            for half in range(2):
                hr = halves[half]
                in_half = half_rows == half
                ks_qs = {h: _dot(jnp.concatenate([kcb[h][hr], qg[h][hr]], axis=0), st[h].astype(BF16))
                         for h in heads}
                for h in heads:
                    v_new = u[h][hr] - ks_qs[h][:GDN_CHUNK]
                    v_full = jnp.where(in_half, jnp.concatenate([v_new, v_new], axis=0), 0.0).astype(BF16)
                    o_halves[h].append(ks_qs[h][GDN_CHUNK:] + _dot(qk[h][hr], v_full))
                    g_last = gcol[h][(half + 1) * GDN_CHUNK - 1:(half + 1) * GDN_CHUNK, :]
                    kdec = jnp.where(in_half, kh[h] * jnp.exp(g_last - gcol[h]), 0.0).astype(BF16)
                    st[h] = st[h] * jnp.exp(g_last) + _dot_tn(kdec, v_full)
            for h in heads:
                st_ref[h] = st[h]
                o = jnp.concatenate(o_halves[h], axis=0)
                o = o * lax.rsqrt(jnp.mean(o * o, axis=-1, keepdims=True) + RMS_EPS) * nw
                zh = z_ref[rows, hsl[h]]
                o_ref[rows, hsl[h]] = (o * _silu(zh)).astype(o_ref.dtype)
        return carry

    lax.fori_loop(0, blk // PAIR, pair_body, 0, unroll=True)


def _gdn_prompt(p1, ba, conv_w, alog_row, dtb_row, nw_row, *, batch, seq, blk):
    nblk = seq // blk
    z_col = CONV_DIM // HEADS_WIDTH
    kern = functools.partial(_gdn_prompt_kernel, blk=blk)
    return pl.pallas_call(
        kern,
        grid=(batch, nblk),
        in_specs=[
            pl.BlockSpec((blk, CONV_DIM), lambda b, j: (b * nblk + j, 0)),
            pl.BlockSpec((blk, HEADS_WIDTH), lambda b, j: (b * nblk + j, z_col)),
            pl.BlockSpec((blk, LANES), lambda b, j: (b * nblk + j, 0)),
            pl.BlockSpec((CONV_W, CONV_DIM), lambda b, j: (0, 0)),
            pl.BlockSpec((1, LANES), lambda b, j: (0, 0)),
            pl.BlockSpec((1, LANES), lambda b, j: (0, 0)),
            pl.BlockSpec((1, HEAD_DIM), lambda b, j: (0, 0)),
        ],
        out_specs=[
            pl.BlockSpec((blk, HEADS_WIDTH), lambda b, j: (b * nblk + j, 0)),
            pl.BlockSpec((None, N_HEADS, HEAD_DIM, HEAD_DIM), lambda b, j: (b, 0, 0, 0)),
        ],
        out_shape=[
            jax.ShapeDtypeStruct((batch * seq, HEADS_WIDTH), BF16),
            jax.ShapeDtypeStruct((batch, N_HEADS, HEAD_DIM, HEAD_DIM), F32),
        ],
        scratch_shapes=[
            pltpu.VMEM((blk + 8, CONV_DIM), F32),
            pltpu.VMEM((blk, HEADS_WIDTH), F32),
            pltpu.VMEM((blk, HEADS_WIDTH), F32),
            pltpu.VMEM((blk, HEADS_WIDTH), F32),
            pltpu.VMEM((blk, LANES), F32),
            pltpu.VMEM((blk, LANES), F32),
        ],
        compiler_params=_cparams(2),
        name="gdn_prompt",
    )(p1, p1, ba, conv_w, alog_row, dtb_row, nw_row)


def _gdn_sample_kernel(u_ref, z_ref, ba_ref, c0_ref, s0_ref, cw_ref, alog_ref, dtb_ref, nw_ref,
                       o_ref, st_ref, cn_ref):
    u = u_ref[0]
    c0 = c0_ref[...]
    acc = c0[0:1] * cw_ref[0:1, :]
    acc = acc + c0[1:2] * cw_ref[1:2, :]
    acc = acc + c0[2:3] * cw_ref[2:3, :]
    acc = acc + u * cw_ref[3:4, :]
    cu = _silu(acc)
    cn_ref[0:1, :] = c0[1:2]
    cn_ref[1:2, :] = c0[2:3]
    cn_ref[2:3, :] = u

    ba = ba_ref[0]
    beta_all = jax.nn.sigmoid(ba)
    g_all = -jnp.exp(alog_ref[...]) * _softplus(ba + dtb_ref[...])
    z = z_ref[0]
    nw = nw_ref[...]
    first_row = lax.broadcasted_iota(jnp.int32, (8, HEAD_DIM), 0) == 0
    for h in range(N_HEADS):
        hs = slice(h * HEAD_DIM, (h + 1) * HEAD_DIM)
        q = cu[:, hs]
        k = cu[:, HEADS_WIDTH + h * HEAD_DIM:HEADS_WIDTH + (h + 1) * HEAD_DIM]
        v = cu[:, 2 * HEADS_WIDTH + h * HEAD_DIM:2 * HEADS_WIDTH + (h + 1) * HEAD_DIM]
        q = q * lax.rsqrt(jnp.sum(q * q, axis=-1, keepdims=True) + RMS_EPS) * (HEAD_DIM ** -0.5)
        k = k * lax.rsqrt(jnp.sum(k * k, axis=-1, keepdims=True) + RMS_EPS)
        beta = beta_all[:, h:h + 1]
        g = g_all[:, N_HEADS + h:N_HEADS + h + 1]
        eg = jnp.exp(g)
        st = s0_ref[h]
        stb = st.astype(BF16)
        kcum = jnp.broadcast_to(k * beta * eg, (8, HEAD_DIM)).astype(BF16)
        qg = jnp.broadcast_to(q * eg, (8, HEAD_DIM)).astype(BF16)
        v_new = v * beta - _dot(kcum, stb)[0:1]
        qk = jnp.sum(q.astype(BF16).astype(F32) * k.astype(BF16).astype(F32), axis=-1, keepdims=True)
        o = _dot(qg, stb)[0:1] + qk * v_new.astype(BF16).astype(F32)
        k8 = jnp.where(first_row, jnp.broadcast_to(k, (8, HEAD_DIM)), 0.0).astype(BF16)
        v8 = jnp.broadcast_to(v_new, (8, HEAD_DIM)).astype(BF16)
        st_ref[h] = st * eg + _dot_tn(k8, v8)
        o = o * lax.rsqrt(jnp.mean(o * o, axis=-1, keepdims=True) + RMS_EPS) * nw
        zh = z[:, hs]
        o_ref[0, :, hs] = (o * _silu(zh)).astype(o_ref.dtype)


def _gdn_sample(conv_in, z, ba, conv0, s0, conv_w, alog_row, dtb_row, nw_row):
    db = conv_in.shape[0]
    row3 = lambda a: a.reshape(db, 1, a.shape[-1])
    full = lambda shape: pl.BlockSpec(shape, lambda b: (0,) * len(shape))
    return pl.pallas_call(
        _gdn_sample_kernel,
        grid=(db,),
        in_specs=[
            pl.BlockSpec((1, 1, CONV_DIM), lambda b: (b, 0, 0)),
            pl.BlockSpec((1, 1, HEADS_WIDTH), lambda b: (b, 0, 0)),
            pl.BlockSpec((1, 1, LANES), lambda b: (b, 0, 0)),
            pl.BlockSpec((None, CONV_W - 1, CONV_DIM), lambda b: (b, 0, 0)),
            pl.BlockSpec((None, N_HEADS, HEAD_DIM, HEAD_DIM), lambda b: (b, 0, 0, 0)),
            full((CONV_W, CONV_DIM)),
            full((1, LANES)),
            full((1, LANES)),
            full((1, HEAD_DIM)),
        ],
        out_specs=[
            pl.BlockSpec((1, 1, HEADS_WIDTH), lambda b: (b, 0, 0)),
            pl.BlockSpec((None, N_HEADS, HEAD_DIM, HEAD_DIM), lambda b: (b, 0, 0, 0)),
            pl.BlockSpec((None, CONV_W - 1, CONV_DIM), lambda b: (b, 0, 0)),
        ],
        out_shape=[
            jax.ShapeDtypeStruct((db, 1, HEADS_WIDTH), F32),
            jax.ShapeDtypeStruct((db, N_HEADS, HEAD_DIM, HEAD_DIM), F32),
            jax.ShapeDtypeStruct((db, CONV_W - 1, CONV_DIM), F32),
        ],
        compiler_params=_cparams(1),
        name="gdn_sample",
    )(row3(conv_in), row3(z), row3(ba), conv0, s0, conv_w, alog_row, dtb_row, nw_row)


def _suffix_matrix():
    j = lax.broadcasted_iota(jnp.int32, (LANES, 2 * LANES), 0)
    s = lax.broadcasted_iota(jnp.int32, (LANES, 2 * LANES), 1)
    return ((j >= s) | (s >= LANES)).astype(BF16)


def _suffix_sums(m, uj):
    cs = _dot(m.astype(BF16), uj)
    return cs[:, :LANES], cs[:, LANES:]


def _sb_prompt_kernel(bias_ref, q_ref, k_ref, v_ref, nw_ref, o_ref, acc_ref, r_ref, *, tq):
    h = pl.program_id(1)
    i = pl.program_id(2)
    bias2 = bias_ref[h] * LOG2E
    uj = _suffix_matrix()
    nsub = tq // LANES

    diag_keys = [pl.ds(pl.multiple_of(i * tq + c * LANES, LANES), LANES) for c in range(nsub)]
    z, valid, sums = {}, {}, {}
    for c in reversed(range(nsub)):
        nrow = tq - c * LANES
        z[c] = _dot_nt(q_ref[c * LANES:tq, :], k_ref[diag_keys[c], :]) + bias2
        valid[c] = (lax.broadcasted_iota(jnp.int32, (nrow, LANES), 1)
                    < lax.broadcasted_iota(jnp.int32, (nrow, LANES), 0))
    for c in reversed(range(nsub)):
        sums[c] = _suffix_sums(jnp.where(valid[c], _neg_softplus2(z[c]), 0.0), uj)
    acc = jnp.zeros((tq, HEAD_DIM), F32)
    run = jnp.zeros((tq, LANES), F32)
    for c in reversed(range(nsub)):
        incl, tot = sums[c]
        pad = jnp.zeros((c * LANES, LANES), F32)
        a = jnp.where(valid[c], jnp.exp2(z[c] + incl + run[c * LANES:]), 0.0).astype(BF16)
        upd = _dot(a, v_ref[diag_keys[c], :])
        acc = acc + (jnp.concatenate([pad, upd], axis=0) if c else upd)
        run = run + (jnp.concatenate([pad, tot], axis=0) if c else tot)
    acc_ref[...] = acc
    r_ref[...] = run

    def body(n, carry):
        ks = pl.ds(pl.multiple_of((i - 1 - n) * tq, tq), tq)
        z = _dot_nt(q_ref[...], k_ref[ks, :]) + bias2
        m = _neg_softplus2(z)
        run = r_ref[...]
        parts = [None] * nsub
        for c in reversed(range(nsub)):
            cs = slice(c * LANES, (c + 1) * LANES)
            incl, tot = _suffix_sums(m[:, cs], uj)
            parts[c] = jnp.exp2(z[:, cs] + incl + run).astype(BF16)
            run = run + tot
        r_ref[...] = run
        acc_ref[...] += _dot(jnp.concatenate(parts, axis=1), v_ref[ks, :])
        return carry

    lax.fori_loop(0, i, body, 0)

    o = acc_ref[...]
    o = o * lax.rsqrt(jnp.mean(o * o, axis=-1, keepdims=True) + RMS_EPS) * nw_ref[pl.ds(h, 1), :]
    o_ref[...] = o.astype(o_ref.dtype)


def _sb_prompt(q, k, v, sb_bias, sb_norm_w, *, batch, seq, tq):
    nq = seq // tq
    kern = functools.partial(_sb_prompt_kernel, tq=tq)
    grid_spec = pltpu.PrefetchScalarGridSpec(
        num_scalar_prefetch=1,
        grid=(batch, N_HEADS, nq),
        in_specs=[
            pl.BlockSpec((tq, HEAD_DIM), lambda b, h, i, bias: (b * nq + i, h)),
            pl.BlockSpec((seq, HEAD_DIM), lambda b, h, i, bias: (b, h)),
            pl.BlockSpec((seq, HEAD_DIM), lambda b, h, i, bias: (b, h)),
            pl.BlockSpec((N_HEADS, HEAD_DIM), lambda b, h, i, bias: (0, 0)),
        ],
        out_specs=pl.BlockSpec((tq, HEAD_DIM), lambda b, h, i, bias: (b * nq + i, h)),
        scratch_shapes=[pltpu.VMEM((tq, HEAD_DIM), F32), pltpu.VMEM((tq, LANES), F32)],
    )
    return pl.pallas_call(
        kern,
        grid_spec=grid_spec,
        out_shape=jax.ShapeDtypeStruct((batch * seq, HEADS_WIDTH), BF16),
        compiler_params=_cparams(3),
        name="sb_prompt",
    )(sb_bias, q, k, v, sb_norm_w)


_PAGE_ROWS = PAGE_SIZE * N_HEADS
_PAGE_CHUNKS = _PAGE_ROWS // LANES


def _own_head_lanes():
    sub = lax.broadcasted_iota(jnp.int32, (N_HEADS, LANES), 0)
    lane = lax.broadcasted_iota(jnp.int32, (N_HEADS, LANES), 1)
    return (lane % N_HEADS) == sub


def _sb_page_scores(q_ref, bias_ref, k_refs):
    qb = q_ref[...].astype(BF16)
    bias2 = bias_ref[...] * LOG2E
    zc = []
    for k_ref in k_refs:
        z = _dot_nt(qb, k_ref[...].reshape(_PAGE_ROWS, HEAD_DIM).astype(BF16))
        zc += [z[:, c * LANES:(c + 1) * LANES] + bias2 for c in range(_PAGE_CHUNKS)]
    return zc


def _sb_page_suffix(zc):
    own = _own_head_lanes()
    wl = lax.broadcasted_iota(jnp.int32, (LANES, 2 * LANES), 0)
    wc = lax.broadcasted_iota(jnp.int32, (LANES, 2 * LANES), 1)
    wsuf = ((wl // N_HEADS >= wc // N_HEADS) | (wc >= LANES)).astype(BF16)
    m = jnp.concatenate([jnp.where(own, _neg_softplus2(zz), 0.0) for zz in zc], axis=0)
    m_hi = m.astype(BF16)
    m_lo = (m - m_hi.astype(F32)).astype(BF16)
    return _dot(m_hi, wsuf) + _dot(m_lo, wsuf)


def _sb_page_fold(zc, cs, v_refs, acc_ref, r_ref):
    own = _own_head_lanes()
    run = r_ref[...]
    acc = acc_ref[...]
    for j, v_ref in enumerate(v_refs):
        parts = [None] * _PAGE_CHUNKS
        for c in reversed(range(_PAGE_CHUNKS)):
            n = j * _PAGE_CHUNKS + c
            rs = slice(n * N_HEADS, (n + 1) * N_HEADS)
            e = zc[n] + cs[rs, :LANES] + run
            parts[c] = jnp.where(own, jnp.exp2(e), 0.0).astype(BF16)
            run = run + cs[rs, LANES:]
        vf = v_ref[...].reshape(_PAGE_ROWS, HEAD_DIM).astype(BF16)
        acc = acc + _dot(jnp.concatenate(parts, axis=1), vf)
    r_ref[...] = run
    acc_ref[...] = acc


def _out_ln_kernel(og_ref, os_ref, x_ref, wg_ref, ws_ref, g_ref, b_ref, o_ref):
    mixed = _dot(og_ref[...].astype(BF16), wg_ref[...]) + _dot(os_ref[...].astype(BF16), ws_ref[...])
    o_ref[...] = _layernorm(DN_ALPHA * x_ref[...] + mixed, g_ref[...], b_ref[...])


def _out_ln(og, osb, x, wg, ws, g, b, *, tm, name):
    m, d = x.shape
    return pl.pallas_call(
        _out_ln_kernel,
        grid=(m // tm,),
        in_specs=[
            pl.BlockSpec((tm, HEADS_WIDTH), lambda i: (i, 0)),
            pl.BlockSpec((tm, HEADS_WIDTH), lambda i: (i, 0)),
            pl.BlockSpec((tm, d), lambda i: (i, 0)),
            pl.BlockSpec((HEADS_WIDTH, d), lambda i: (0, 0)),
            pl.BlockSpec((HEADS_WIDTH, d), lambda i: (0, 0)),
            pl.BlockSpec((1, d), lambda i: (0, 0)),
            pl.BlockSpec((1, d), lambda i: (0, 0)),
        ],
        out_specs=pl.BlockSpec((tm, d), lambda i: (i, 0)),
        out_shape=jax.ShapeDtypeStruct((m, d), F32),
        compiler_params=_cparams(1),
        name=name,
    )(og, osb, x, wg, ws, g, b)


def _ffn_kernel(*refs, pages_per_step, steps_per_seq):
    if pages_per_step:
        _, h_ref, wu_ref, wd_ref, g_ref, b_ref, q_ref, bias_ref, nw_ref = refs[:9]
        k_refs = refs[9:9 + pages_per_step]
        v_refs = refs[9 + pages_per_step:9 + 2 * pages_per_step]
        o_ref, os_ref, hb_ref, acc_ref, r_ref = refs[9 + 2 * pages_per_step:]
    else:
        h_ref, wu_ref, wd_ref, g_ref, b_ref, o_ref, hb_ref = refs
    f = pl.program_id(1)
    n_f = pl.num_programs(1)

    @pl.when(f == 0)
    def _():
        hb_ref[...] = h_ref[...].astype(BF16)
        o_ref[...] = jnp.zeros_like(o_ref)

    if pages_per_step:
        s = (pl.program_id(0) * n_f + f) % steps_per_seq

        @pl.when(s == 0)
        def _():
            acc_ref[...] = jnp.zeros_like(acc_ref)
            r_ref[...] = jnp.zeros_like(r_ref)

    if pages_per_step:
        zc = _sb_page_scores(q_ref, bias_ref, k_refs)
    a = jnp.maximum(_dot(hb_ref[...], wu_ref[...]), 0.0)
    if pages_per_step:
        cs = _sb_page_suffix(zc)
    o_ref[...] += _dot((a * a).astype(BF16), wd_ref[...])
    if pages_per_step:
        _sb_page_fold(zc, cs, v_refs, acc_ref, r_ref)

        @pl.when(s == steps_per_seq - 1)
        def _():
            o = acc_ref[...]
            o = o * lax.rsqrt(jnp.mean(o * o, axis=-1, keepdims=True) + RMS_EPS) * nw_ref[...]
            os_ref[...] = o.astype(os_ref.dtype)

    @pl.when(f == n_f - 1)
    def _():
        o_ref[...] = _layernorm(DN_ALPHA * h_ref[...] + o_ref[...], g_ref[...], b_ref[...])


def _ffn(h, wu, wd, g, b, *, tm, name, attn=None):
    m, d = h.shape
    tf = FFN_TILE
    n_m, n_f = m // tm, wu.shape[1] // tf
    in_specs = [
        pl.BlockSpec((tm, d), lambda i, f, *_: (i, 0)),
        pl.BlockSpec((d, tf), lambda i, f, *_: (0, f)),
        pl.BlockSpec((tf, d), lambda i, f, *_: (f, 0)),
        pl.BlockSpec((1, d), lambda i, f, *_: (0, 0)),
        pl.BlockSpec((1, d), lambda i, f, *_: (0, 0)),
    ]
    out_specs = [pl.BlockSpec((tm, d), lambda i, f, *_: (i, 0))]
    out_shape = [jax.ShapeDtypeStruct((m, d), F32)]
    scratch = [pltpu.VMEM((tm, d), BF16)]
    args = [h, wu, wd, g, b]
    if attn is None:
        pages_per_step = steps_per_seq = 0
        prefetch = []
    else:
        q, cache_k, cache_v, page_table, bias_tile, sb_norm_w, pages_per_step = attn
        db, n_pages = page_table.shape
        steps_per_seq = n_pages // pages_per_step
        assert n_m * n_f == db * steps_per_seq, "attention steps must tile the FFN grid exactly"
        seq_of = lambda i, f: (i * n_f + f) // steps_per_seq
        first_page = lambda i, f: n_pages - 1 - ((i * n_f + f) % steps_per_seq) * pages_per_step

        def page_spec(j):
            return pl.BlockSpec((None, None, PAGE_SIZE, N_HEADS, HEAD_DIM),
                                lambda i, f, pt: (0, pt[seq_of(i, f), first_page(i, f) - j], 0, 0, 0))

        in_specs += [
            pl.BlockSpec((None, N_HEADS, HEAD_DIM), lambda i, f, pt: (seq_of(i, f), 0, 0)),
            pl.BlockSpec((N_HEADS, LANES), lambda i, f, pt: (0, 0)),
            pl.BlockSpec((N_HEADS, HEAD_DIM), lambda i, f, pt: (0, 0)),
        ] + [page_spec(j) for j in range(pages_per_step)] * 2
        out_specs.append(pl.BlockSpec((None, N_HEADS, HEAD_DIM), lambda i, f, pt: (seq_of(i, f), 0, 0)))
        out_shape.append(jax.ShapeDtypeStruct((db, N_HEADS, HEAD_DIM), F32))
        scratch += [pltpu.VMEM((N_HEADS, HEAD_DIM), F32), pltpu.VMEM((N_HEADS, LANES), F32)]
        args += [q, bias_tile, sb_norm_w] + [cache_k] * pages_per_step + [cache_v] * pages_per_step
        prefetch = [page_table]
    outs = pl.pallas_call(
        functools.partial(_ffn_kernel, pages_per_step=pages_per_step, steps_per_seq=steps_per_seq),
        grid_spec=pltpu.PrefetchScalarGridSpec(
            num_scalar_prefetch=len(prefetch), grid=(n_m, n_f),
            in_specs=in_specs, out_specs=out_specs, scratch_shapes=scratch),
        out_shape=out_shape,
        compiler_params=_cparams(2),
        name=name,
    )(*prefetch, *args)
    return outs[0] if attn is None else outs


def kernel(x_prompt, x_sample, cache_k, cache_v, state_gdn, state_conv, page_table, w_in, conv_w, a_log,
           dt_bias, gdn_norm_w, sb_norm_w, sb_bias, w_out, ln1_g, ln1_b, w_up, w_down, ln2_g, ln2_b):
    bp, sp, d = x_prompt.shape
    db = x_sample.shape[0]
    assert w_in.shape[0] == 1 and x_sample.shape[1] == 1

    w = w_in[0]
    o_z = CONV_DIM
    o_b = o_z + HEADS_WIDTH
    o_a = o_b + N_HEADS
    o_q = o_a + N_HEADS
    o_k = o_q + HEADS_WIDTH
    o_v = o_k + HEADS_WIDTH
    w_main = w[:, :o_b].astype(BF16)
    w_k = w[:, o_k:o_v].astype(BF16)
    w_v = w[:, o_v:].astype(BF16)
    w_qba = jnp.concatenate(
        [w[:, o_q:o_k], w[:, o_b:o_q], jnp.zeros((d, LANES - 2 * N_HEADS), w.dtype)], axis=1).astype(BF16)
    w_og = w_out[0, :HEADS_WIDTH].astype(BF16)
    w_os = w_out[0, HEADS_WIDTH:].astype(BF16)
    wu = w_up[0].astype(BF16)
    wd = w_down[0].astype(BF16)
    pad_row = lambda v: jnp.pad(v, (N_HEADS, LANES - 2 * N_HEADS)).reshape(1, LANES)
    alog_row = pad_row(a_log[0])
    dtb_row = pad_row(dt_bias[0])
    nw_row = gdn_norm_w[0].reshape(1, HEAD_DIM)
    cw = conv_w[0]
    row = lambda v: v.reshape(1, d)
    bias_tile = jnp.broadcast_to(sb_bias[0][:, None], (N_HEADS, LANES))

    xp = x_prompt.reshape(bp * sp, d)
    p1 = _matmul(xp, w_main, tm=1024, tn=1024, name="proj_main_p")
    sqp, bap = _matmul_split(xp, w_qba, tm=1024, name="proj_qba_p",
                             outs=[(HEADS_WIDTH, BF16, SB_QSCALE), (LANES, F32, None)])
    skp, skp_b = _matmul(xp, w_k, tm=1024, tn=1024, name="proj_k_p", out_dtypes=(F32, BF16))
    svp, svp_b = _matmul(xp, w_v, tm=1024, tn=1024, name="proj_v_p", out_dtypes=(F32, BF16))
    og_p, gdn_p = _gdn_prompt(p1, bap, cw, alog_row, dtb_row, nw_row, batch=bp, seq=sp, blk=256)
    os_p = _sb_prompt(sqp, skp_b, svp_b, sb_bias[0], sb_norm_w[0], batch=bp, seq=sp, tq=512)
    h_p = _out_ln(og_p, os_p, xp, w_og, w_os, row(ln1_g[0]), row(ln1_b[0]), tm=512, name="out_ln_p")
    conv_p = p1.reshape(bp, sp, -1)[:, sp - (CONV_W - 1):, :CONV_DIM]

    xs = x_sample.reshape(db, d)
    s1 = _matmul(xs, w_main, tm=db, tn=1024, name="proj_main_s")
    sqs, bas = _matmul_split(xs, w_qba, tm=db, name="proj_qba_s",
                             outs=[(HEADS_WIDTH, F32, SB_QSCALE), (LANES, F32, None)])
    sks = _matmul(xs, w_k, tm=db, tn=1024, name="proj_k_s")
    svs = _matmul(xs, w_v, tm=db, tn=1024, name="proj_v_s")
    og_s, gdn_s, conv_s = _gdn_sample(s1[:, :CONV_DIM], s1[:, CONV_DIM:], bas,
                                      state_conv[0], state_gdn[0], cw, alog_row, dtb_row, nw_row)
    y_p, os_s = _ffn(h_p, wu, wd, row(ln2_g[0]), row(ln2_b[0]), tm=512, name="ffn_p",
                     attn=(sqs.reshape(db, N_HEADS, HEAD_DIM), cache_k, cache_v, page_table, bias_tile,
                           sb_norm_w[0], 8))
    h_s = _out_ln(og_s.reshape(db, HEADS_WIDTH), os_s.reshape(db, HEADS_WIDTH), xs, w_og, w_os,
                  row(ln1_g[0]), row(ln1_b[0]), tm=db, name="out_ln_s")
    y_s = _ffn(h_s, wu, wd, row(ln2_g[0]), row(ln2_b[0]), tm=db, name="ffn_s")

    n_pg = sp // PAGE_SIZE
    return (
        y_p.reshape(bp, sp, d),
        y_s.reshape(db, 1, d),
        skp.reshape(1, bp, n_pg, PAGE_SIZE, N_HEADS, HEAD_DIM),
        svp.reshape(1, bp, n_pg, PAGE_SIZE, N_HEADS, HEAD_DIM),
        gdn_p[None],
        conv_p[None],
        sks.reshape(1, db, 1, N_HEADS, HEAD_DIM),
        svs.reshape(1, db, 1, N_HEADS, HEAD_DIM),
        gdn_s[None],
        conv_s[None],
    )
```

```python
import functools

import jax
import jax.numpy as jnp
from jax import lax
from jax.experimental import pallas as pl
from jax.experimental.pallas import tpu as pltpu

F32 = jnp.float32
BF16 = jnp.bfloat16

LANES = 128
HEAD_DIM = 128
N_HEADS = 8
HEADS_WIDTH = N_HEADS * HEAD_DIM
CONV_W = 4
CONV_DIM = 3 * HEADS_WIDTH
GDN_CHUNK = 64
PAIR = 2 * GDN_CHUNK
PAGE_SIZE = 128
LN_EPS = 1e-5
RMS_EPS = 1e-6
DN_ALPHA = 2.0 ** 0.25
VMEM_LIMIT = 52 * 1024 * 1024
FFN_TILE = 512
HEAD_GROUP = 8
LOG2E = 1.4426950408889634
SB_QSCALE = HEAD_DIM ** -0.5 * LOG2E

_HI = lax.Precision.HIGHEST


def _cparams(n_axes):
    return pltpu.CompilerParams(dimension_semantics=("arbitrary",) * n_axes,
                                vmem_limit_bytes=VMEM_LIMIT)


def _softplus(x):
    return jnp.maximum(x, 0.0) + jnp.log1p(jnp.exp(-jnp.abs(x)))


def _silu(x):
    h = 0.5 * x
    return h + h * jnp.tanh(h)


def _neg_softplus2(z2):
    return jnp.minimum(-z2, 0.0) - jnp.log2(1.0 + jnp.exp2(-jnp.abs(z2)))


def _dot(a, b):
    return jnp.dot(a, b, preferred_element_type=F32)


def _dot_nt(a, b):
    return lax.dot_general(a, b, (((1,), (1,)), ((), ())), preferred_element_type=F32)


def _dot_tn(a, b):
    return lax.dot_general(a, b, (((0,), (0,)), ((), ())), preferred_element_type=F32)


def _dot_hi(a, b):
    return jnp.dot(a, b, preferred_element_type=F32, precision=_HI)


def _layernorm(pre, g, b):
    mu = jnp.mean(pre, axis=-1, keepdims=True)
    d = pre - mu
    var = jnp.mean(d * d, axis=-1, keepdims=True)
    return d * lax.rsqrt(var + LN_EPS) * g + b


def _mm_kernel(x_ref, w_ref, *o_refs):
    acc = _dot(x_ref[...].astype(BF16), w_ref[...])
    for o_ref in o_refs:
        o_ref[...] = acc.astype(o_ref.dtype)


def _matmul(x, w, *, tm, tn, name, out_dtypes=(F32,)):
    m, k = x.shape
    n = w.shape[1]
    outs = pl.pallas_call(
        _mm_kernel,
        grid=(n // tn, m // tm),
        in_specs=[pl.BlockSpec((tm, k), lambda j, i: (i, 0)),
                  pl.BlockSpec((k, tn), lambda j, i: (0, j))],
        out_specs=[pl.BlockSpec((tm, tn), lambda j, i: (i, j)) for _ in out_dtypes],
        out_shape=[jax.ShapeDtypeStruct((m, n), dt) for dt in out_dtypes],
        compiler_params=_cparams(2),
        name=name,
    )(x, w)
    return outs if len(outs) > 1 else outs[0]


def _mm_split_kernel(x_ref, w_ref, *o_refs, scales):
    acc = _dot(x_ref[...].astype(BF16), w_ref[...])
    col = 0
    for o_ref, scale in zip(o_refs, scales):
        part = acc[:, col:col + o_ref.shape[-1]]
        col += o_ref.shape[-1]
        o_ref[...] = (part if scale is None else part * scale).astype(o_ref.dtype)


def _matmul_split(x, w, *, tm, name, outs):
    m, k = x.shape
    n = w.shape[1]
    assert n == sum(width for width, _, _ in outs)
    return pl.pallas_call(
        functools.partial(_mm_split_kernel, scales=tuple(s for _, _, s in outs)),
        grid=(m // tm,),
        in_specs=[pl.BlockSpec((tm, k), lambda i: (i, 0)),
                  pl.BlockSpec((k, n), lambda i: (0, 0))],
        out_specs=[pl.BlockSpec((tm, width), lambda i: (i, 0)) for width, _, _ in outs],
        out_shape=[jax.ShapeDtypeStruct((m, width), dt) for width, dt, _ in outs],
        compiler_params=_cparams(1),
        name=name,
    )(x, w)


def _gdn_prompt_kernel(conv_ref, z_ref, ba_ref, cw_ref, alog_ref, dtb_ref, nw_ref,
                       o_ref, st_ref,
                       ext_ref, q_s, k_s, v_s, beta_s, g_s, *, blk):
    j = pl.program_id(1)

    @pl.when(j == 0)
    def _():
        ext_ref[0:8, :] = jnp.zeros((8, CONV_DIM), F32)
        st_ref[...] = jnp.zeros_like(st_ref)

    @pl.when(j > 0)
    def _():
        ext_ref[0:8, :] = ext_ref[blk:blk + 8, :]

    ext_ref[8:8 + blk, :] = conv_ref[...]

    for c in range(CONV_DIM // LANES):
        cs = slice(c * LANES, (c + 1) * LANES)
        e = ext_ref[:, cs]
        acc = pltpu.roll(e, CONV_W - 1, axis=0)[8:] * cw_ref[0:1, cs]
        for k in range(1, CONV_W - 1):
            acc = acc + pltpu.roll(e, CONV_W - 1 - k, axis=0)[8:] * cw_ref[k:k + 1, cs]
        acc = acc + e[8:] * cw_ref[CONV_W - 1:CONV_W, cs]
        cu = _silu(acc)
        if c < 2 * N_HEADS:
            ss = jnp.sum(cu * cu, axis=-1, keepdims=True)
            cu = cu * lax.rsqrt(ss + RMS_EPS)
        if c < N_HEADS:
            q_s[:, cs] = cu * (HEAD_DIM ** -0.5)
        elif c < 2 * N_HEADS:
            k_s[:, (c - N_HEADS) * LANES:(c - N_HEADS + 1) * LANES] = cu
        else:
            v_s[:, (c - 2 * N_HEADS) * LANES:(c - 2 * N_HEADS + 1) * LANES] = cu

    ba = ba_ref[...]
    beta_s[...] = jax.nn.sigmoid(ba)
    g_s[...] = -jnp.exp(alog_ref[...]) * _softplus(ba + dtb_ref[...])

    row = lax.broadcasted_iota(jnp.int32, (PAIR, PAIR), 0)
    col = lax.broadcasted_iota(jnp.int32, (PAIR, PAIR), 1)
    same = (row // GDN_CHUNK) == (col // GDN_CHUNK)
    tri_incl = same & (row >= col)
    tri_strict = same & (row > col)
    cum_l = tri_incl.astype(F32)
    nw = nw_ref[...]
    half_rows = lax.broadcasted_iota(jnp.int32, (PAIR, 1), 0) // GDN_CHUNK
    halves = [slice(half * GDN_CHUNK, (half + 1) * GDN_CHUNK) for half in range(2)]

    def pair_body(p, carry):
        r0 = pl.multiple_of(p * PAIR, PAIR)
        rows = pl.ds(r0, PAIR)
        g_p = g_s[rows, :]
        beta_p = beta_s[rows, :]
        gc_all = _dot_hi(cum_l, g_p)
        gct_all = jnp.transpose(gc_all)
        for h0 in range(0, N_HEADS, HEAD_GROUP):
            heads = range(h0, h0 + HEAD_GROUP)
            hsl = {h: slice(h * HEAD_DIM, (h + 1) * HEAD_DIM) for h in heads}
            kh = {h: k_s[rows, hsl[h]] for h in heads}
            gcol = {h: gc_all[:, N_HEADS + h:N_HEADS + h + 1] for h in heads}
            gexp = {h: jnp.exp(gcol[h]) for h in heads}
            khb = {h: kh[h].astype(BF16) for h in heads}
            kb = {h: kh[h] * beta_p[:, h:h + 1] for h in heads}
            decay, xp, tm = {}, {}, {}
            for h in heads:
                grow = gct_all[N_HEADS + h:N_HEADS + h + 1, :]
                decay[h] = jnp.exp(jnp.where(tri_incl, gcol[h] - grow, -jnp.inf))
                lower = jnp.where(tri_strict, _dot_nt(kb[h].astype(BF16), khb[h]) * decay[h], 0.0)
                xp[h] = -lower
                tm[h] = xp[h]
            for _ in range(5):
                for h in heads:
                    xb = xp[h].astype(BF16)
                    xp[h] = _dot(xb, xb)
                for h in heads:
                    tm[h] = tm[h] + xp[h] + _dot(tm[h].astype(BF16), xp[h].astype(BF16))
            u, kcb, qk, qg = {}, {}, {}, {}
            for h in heads:
                rhs = jnp.concatenate([v_s[rows, hsl[h]] * beta_p[:, h:h + 1], kb[h] * gexp[h]], axis=1)
                uk = rhs + _dot(tm[h].astype(BF16), rhs.astype(BF16))
                u[h] = uk[:, :HEAD_DIM]
                kcb[h] = uk[:, HEAD_DIM:].astype(BF16)
                qh = q_s[rows, hsl[h]]
                qk[h] = jnp.where(tri_incl, _dot_nt(qh.astype(BF16), khb[h]) * decay[h], 0.0).astype(BF16)
                qg[h] = (qh * gexp[h]).astype(BF16)
            st = {h: st_ref[h] for h in heads}
            o_halves = {h: [] for h in heads}
            for half in range(2):
                hr = halves[half]
                in_half = half_rows == half
                ks_qs = {h: _dot(jnp.concatenate([kcb[h][hr], qg[h][hr]], axis=0), st[h].astype(BF16))
                         for h in heads}
                for h in heads:
                    v_new = u[h][hr] - ks_qs[h][:GDN_CHUNK]
                    v_full = jnp.where(in_half, jnp.concatenate([v_new, v_new], axis=0), 0.0).astype(BF16)
                    o_halves[h].append(ks_qs[h][GDN_CHUNK:] + _dot(qk[h][hr], v_full))
                    g_last = gcol[h][(half + 1) * GDN_CHUNK - 1:(half + 1) * GDN_CHUNK, :]
                    kdec = jnp.where(in_half, kh[h] * jnp.exp(g_last - gcol[h]), 0.0).astype(BF16)
                    st[h] = st[h] * jnp.exp(g_last) + _dot_tn(kdec, v_full)
            for h in heads:
                st_ref[h] = st[h]
                o = jnp.concatenate(o_halves[h], axis=0)
                o = o * lax.rsqrt(jnp.mean(o * o, axis=-1, keepdims=True) + RMS_EPS) * nw
                zh = z_ref[rows, hsl[h]]
                o_ref[rows, hsl[h]] = (o * _silu(zh)).astype(o_ref.dtype)
        return carry

    lax.fori_loop(0, blk // PAIR, pair_body, 0, unroll=True)


def _gdn_prompt(p1, ba, conv_w, alog_row, dtb_row, nw_row, *, batch, seq, blk):
    nblk = seq // blk
    z_col = CONV_DIM // HEADS_WIDTH
    kern = functools.partial(_gdn_prompt_kernel, blk=blk)
    return pl.pallas_call(
        kern,
        grid=(batch, nblk),
        in_specs=[
            pl.BlockSpec((blk, CONV_DIM), lambda b, j: (b * nblk + j, 0)),
            pl.BlockSpec((blk, HEADS_WIDTH), lambda b, j: (b * nblk + j, z_col)),
            pl.BlockSpec((blk, LANES), lambda b, j: (b * nblk + j, 0)),
            pl.BlockSpec((CONV_W, CONV_DIM), lambda b, j: (0, 0)),
            pl.BlockSpec((1, LANES), lambda b, j: (0, 0)),
            pl.BlockSpec((1, LANES), lambda b, j: (0, 0)),
            pl.BlockSpec((1, HEAD_DIM), lambda b, j: (0, 0)),
        ],
        out_specs=[
            pl.BlockSpec((blk, HEADS_WIDTH), lambda b, j: (b * nblk + j, 0)),
            pl.BlockSpec((None, N_HEADS, HEAD_DIM, HEAD_DIM), lambda b, j: (b, 0, 0, 0)),
        ],
        out_shape=[
            jax.ShapeDtypeStruct((batch * seq, HEADS_WIDTH), BF16),
            jax.ShapeDtypeStruct((batch, N_HEADS, HEAD_DIM, HEAD_DIM), F32),
        ],
        scratch_shapes=[
            pltpu.VMEM((blk + 8, CONV_DIM), F32),
            pltpu.VMEM((blk, HEADS_WIDTH), F32),
            pltpu.VMEM((blk, HEADS_WIDTH), F32),
            pltpu.VMEM((blk, HEADS_WIDTH), F32),
            pltpu.VMEM((blk, LANES), F32),
            pltpu.VMEM((blk, LANES), F32),
        ],
        compiler_params=_cparams(2),
        name="gdn_prompt",
    )(p1, p1, ba, conv_w, alog_row, dtb_row, nw_row)


def _gdn_sample_kernel(u_ref, z_ref, ba_ref, c0_ref, s0_ref, cw_ref, alog_ref, dtb_ref, nw_ref,
                       o_ref, st_ref, cn_ref):
    u = u_ref[0]
    c0 = c0_ref[...]
    acc = c0[0:1] * cw_ref[0:1, :]
    acc = acc + c0[1:2] * cw_ref[1:2, :]
    acc = acc + c0[2:3] * cw_ref[2:3, :]
    acc = acc + u * cw_ref[3:4, :]
    cu = _silu(acc)
    cn_ref[0:1, :] = c0[1:2]
    cn_ref[1:2, :] = c0[2:3]
    cn_ref[2:3, :] = u

    ba = ba_ref[0]
    beta_all = jax.nn.sigmoid(ba)
    g_all = -jnp.exp(alog_ref[...]) * _softplus(ba + dtb_ref[...])
    z = z_ref[0]
    nw = nw_ref[...]
    first_row = lax.broadcasted_iota(jnp.int32, (8, HEAD_DIM), 0) == 0
    for h in range(N_HEADS):
        hs = slice(h * HEAD_DIM, (h + 1) * HEAD_DIM)
        q = cu[:, hs]
        k = cu[:, HEADS_WIDTH + h * HEAD_DIM:HEADS_WIDTH + (h + 1) * HEAD_DIM]
        v = cu[:, 2 * HEADS_WIDTH + h * HEAD_DIM:2 * HEADS_WIDTH + (h + 1) * HEAD_DIM]
        q = q * lax.rsqrt(jnp.sum(q * q, axis=-1, keepdims=True) + RMS_EPS) * (HEAD_DIM ** -0.5)
        k = k * lax.rsqrt(jnp.sum(k * k, axis=-1, keepdims=True) + RMS_EPS)
        beta = beta_all[:, h:h + 1]
        g = g_all[:, N_HEADS + h:N_HEADS + h + 1]
        eg = jnp.exp(g)
        st = s0_ref[h]
        stb = st.astype(BF16)
        kcum = jnp.broadcast_to(k * beta * eg, (8, HEAD_DIM)).astype(BF16)
        qg = jnp.broadcast_to(q * eg, (8, HEAD_DIM)).astype(BF16)
        v_new = v * beta - _dot(kcum, stb)[0:1]
        qk = jnp.sum(q.astype(BF16).astype(F32) * k.astype(BF16).astype(F32), axis=-1, keepdims=True)
        o = _dot(qg, stb)[0:1] + qk * v_new.astype(BF16).astype(F32)
        k8 = jnp.where(first_row, jnp.broadcast_to(k, (8, HEAD_DIM)), 0.0).astype(BF16)
        v8 = jnp.broadcast_to(v_new, (8, HEAD_DIM)).astype(BF16)
        st_ref[h] = st * eg + _dot_tn(k8, v8)
        o = o * lax.rsqrt(jnp.mean(o * o, axis=-1, keepdims=True) + RMS_EPS) * nw
        zh = z[:, hs]
        o_ref[0, :, hs] = (o * _silu(zh)).astype(o_ref.dtype)


def _gdn_sample(conv_in, z, ba, conv0, s0, conv_w, alog_row, dtb_row, nw_row):
    db = conv_in.shape[0]
    row3 = lambda a: a.reshape(db, 1, a.shape[-1])
    full = lambda shape: pl.BlockSpec(shape, lambda b: (0,) * len(shape))
    return pl.pallas_call(
        _gdn_sample_kernel,
        grid=(db,),
        in_specs=[
            pl.BlockSpec((1, 1, CONV_DIM), lambda b: (b, 0, 0)),
            pl.BlockSpec((1, 1, HEADS_WIDTH), lambda b: (b, 0, 0)),
            pl.BlockSpec((1, 1, LANES), lambda b: (b, 0, 0)),
            pl.BlockSpec((None, CONV_W - 1, CONV_DIM), lambda b: (b, 0, 0)),
            pl.BlockSpec((None, N_HEADS, HEAD_DIM, HEAD_DIM), lambda b: (b, 0, 0, 0)),
            full((CONV_W, CONV_DIM)),
            full((1, LANES)),
            full((1, LANES)),
            full((1, HEAD_DIM)),
        ],
        out_specs=[
            pl.BlockSpec((1, 1, HEADS_WIDTH), lambda b: (b, 0, 0)),
            pl.BlockSpec((None, N_HEADS, HEAD_DIM, HEAD_DIM), lambda b: (b, 0, 0, 0)),
            pl.BlockSpec((None, CONV_W - 1, CONV_DIM), lambda b: (b, 0, 0)),
        ],
        out_shape=[
            jax.ShapeDtypeStruct((db, 1, HEADS_WIDTH), F32),
            jax.ShapeDtypeStruct((db, N_HEADS, HEAD_DIM, HEAD_DIM), F32),
            jax.ShapeDtypeStruct((db, CONV_W - 1, CONV_DIM), F32),
        ],
        compiler_params=_cparams(1),
        name="gdn_sample",
    )(row3(conv_in), row3(z), row3(ba), conv0, s0, conv_w, alog_row, dtb_row, nw_row)


def _suffix_matrix():
    j = lax.broadcasted_iota(jnp.int32, (LANES, 2 * LANES), 0)
    s = lax.broadcasted_iota(jnp.int32, (LANES, 2 * LANES), 1)
    return ((j >= s) | (s >= LANES)).astype(BF16)


def _suffix_sums(m, uj):
    cs = _dot(m.astype(BF16), uj)
    return cs[:, :LANES], cs[:, LANES:]


def _sb_prompt_kernel(bias_ref, q_ref, k_ref, v_ref, nw_ref, o_ref, acc_ref, r_ref, *, tq):
    h = pl.program_id(1)
    i = pl.program_id(2)
    bias2 = bias_ref[h] * LOG2E
    uj = _suffix_matrix()
    nsub = tq // LANES

    diag_keys = [pl.ds(pl.multiple_of(i * tq + c * LANES, LANES), LANES) for c in range(nsub)]
    z, valid, sums = {}, {}, {}
    for c in reversed(range(nsub)):
        nrow = tq - c * LANES
        z[c] = _dot_nt(q_ref[c * LANES:tq, :], k_ref[diag_keys[c], :]) + bias2
        valid[c] = (lax.broadcasted_iota(jnp.int32, (nrow, LANES), 1)
                    < lax.broadcasted_iota(jnp.int32, (nrow, LANES), 0))
    for c in reversed(range(nsub)):
        sums[c] = _suffix_sums(jnp.where(valid[c], _neg_softplus2(z[c]), 0.0), uj)
    acc = jnp.zeros((tq, HEAD_DIM), F32)
    run = jnp.zeros((tq, LANES), F32)
    for c in reversed(range(nsub)):
        incl, tot = sums[c]
        pad = jnp.zeros((c * LANES, LANES), F32)
        a = jnp.where(valid[c], jnp.exp2(z[c] + incl + run[c * LANES:]), 0.0).astype(BF16)
        upd = _dot(a, v_ref[diag_keys[c], :])
        acc = acc + (jnp.concatenate([pad, upd], axis=0) if c else upd)
        run = run + (jnp.concatenate([pad, tot], axis=0) if c else tot)
    acc_ref[...] = acc
    r_ref[...] = run

    def body(n, carry):
        ks = pl.ds(pl.multiple_of((i - 1 - n) * tq, tq), tq)
        z = _dot_nt(q_ref[...], k_ref[ks, :]) + bias2
        m = _neg_softplus2(z)
        run = r_ref[...]
        parts = [None] * nsub
        for c in reversed(range(nsub)):
            cs = slice(c * LANES, (c + 1) * LANES)
            incl, tot = _suffix_sums(m[:, cs], uj)
            parts[c] = jnp.exp2(z[:, cs] + incl + run).astype(BF16)
            run = run + tot
        r_ref[...] = run
        acc_ref[...] += _dot(jnp.concatenate(parts, axis=1), v_ref[ks, :])
        return carry

    lax.fori_loop(0, i, body, 0)

    o = acc_ref[...]
    o = o * lax.rsqrt(jnp.mean(o * o, axis=-1, keepdims=True) + RMS_EPS) * nw_ref[pl.ds(h, 1), :]
    o_ref[...] = o.astype(o_ref.dtype)


def _sb_prompt(q, k, v, sb_bias, sb_norm_w, *, batch, seq, tq):
    nq = seq // tq
    kern = functools.partial(_sb_prompt_kernel, tq=tq)
    grid_spec = pltpu.PrefetchScalarGridSpec(
        num_scalar_prefetch=1,
        grid=(batch, N_HEADS, nq),
        in_specs=[
            pl.BlockSpec((tq, HEAD_DIM), lambda b, h, i, bias: (b * nq + i, h)),
            pl.BlockSpec((seq, HEAD_DIM), lambda b, h, i, bias: (b, h)),
            pl.BlockSpec((seq, HEAD_DIM), lambda b, h, i, bias: (b, h)),
            pl.BlockSpec((N_HEADS, HEAD_DIM), lambda b, h, i, bias: (0, 0)),
        ],
        out_specs=pl.BlockSpec((tq, HEAD_DIM), lambda b, h, i, bias: (b * nq + i, h)),
        scratch_shapes=[pltpu.VMEM((tq, HEAD_DIM), F32), pltpu.VMEM((tq, LANES), F32)],
    )
    return pl.pallas_call(
        kern,
        grid_spec=grid_spec,
        out_shape=jax.ShapeDtypeStruct((batch * seq, HEADS_WIDTH), BF16),
        compiler_params=_cparams(3),
        name="sb_prompt",
    )(sb_bias, q, k, v, sb_norm_w)


_PAGE_ROWS = PAGE_SIZE * N_HEADS
_PAGE_CHUNKS = _PAGE_ROWS // LANES


def _own_head_lanes():
    sub = lax.broadcasted_iota(jnp.int32, (N_HEADS, LANES), 0)
    lane = lax.broadcasted_iota(jnp.int32, (N_HEADS, LANES), 1)
    return (lane % N_HEADS) == sub


def _sb_page_scores(q_ref, bias_ref, k_refs):
    qb = q_ref[...].astype(BF16)
    bias2 = bias_ref[...] * LOG2E
    zc = []
    for k_ref in k_refs:
        z = _dot_nt(qb, k_ref[...].reshape(_PAGE_ROWS, HEAD_DIM).astype(BF16))
        zc += [z[:, c * LANES:(c + 1) * LANES] + bias2 for c in range(_PAGE_CHUNKS)]
    return zc


def _sb_page_suffix(zc):
    own = _own_head_lanes()
    wl = lax.broadcasted_iota(jnp.int32, (LANES, 2 * LANES), 0)
    wc = lax.broadcasted_iota(jnp.int32, (LANES, 2 * LANES), 1)
    wsuf = ((wl // N_HEADS >= wc // N_HEADS) | (wc >= LANES)).astype(BF16)
    m = jnp.concatenate([jnp.where(own, _neg_softplus2(zz), 0.0) for zz in zc], axis=0)
    return _dot(m.astype(BF16), wsuf)


def _sb_page_fold(zc, cs, v_refs, acc_ref, r_ref):
    own = _own_head_lanes()
    run = r_ref[...]
    acc = acc_ref[...]
    for j, v_ref in enumerate(v_refs):
        parts = [None] * _PAGE_CHUNKS
        for c in reversed(range(_PAGE_CHUNKS)):
            n = j * _PAGE_CHUNKS + c
            rs = slice(n * N_HEADS, (n + 1) * N_HEADS)
            e = zc[n] + cs[rs, :LANES] + run
            parts[c] = jnp.where(own, jnp.exp2(e), 0.0).astype(BF16)
            run = run + cs[rs, LANES:]
        vf = v_ref[...].reshape(_PAGE_ROWS, HEAD_DIM).astype(BF16)
        acc = acc + _dot(jnp.concatenate(parts, axis=1), vf)
    r_ref[...] = run
    acc_ref[...] = acc


def _out_ln_kernel(og_ref, os_ref, x_ref, wg_ref, ws_ref, g_ref, b_ref, o_ref):
    mixed = _dot(og_ref[...].astype(BF16), wg_ref[...]) + _dot(os_ref[...].astype(BF16), ws_ref[...])
    o_ref[...] = _layernorm(DN_ALPHA * x_ref[...] + mixed, g_ref[...], b_ref[...])


def _out_ln(og, osb, x, wg, ws, g, b, *, tm, name):
    m, d = x.shape
    return pl.pallas_call(
        _out_ln_kernel,
        grid=(m // tm,),
        in_specs=[
            pl.BlockSpec((tm, HEADS_WIDTH), lambda i: (i, 0)),
            pl.BlockSpec((tm, HEADS_WIDTH), lambda i: (i, 0)),
            pl.BlockSpec((tm, d), lambda i: (i, 0)),
            pl.BlockSpec((HEADS_WIDTH, d), lambda i: (0, 0)),
            pl.BlockSpec((HEADS_WIDTH, d), lambda i: (0, 0)),
            pl.BlockSpec((1, d), lambda i: (0, 0)),
            pl.BlockSpec((1, d), lambda i: (0, 0)),
        ],
        out_specs=pl.BlockSpec((tm, d), lambda i: (i, 0)),
        out_shape=jax.ShapeDtypeStruct((m, d), F32),
        compiler_params=_cparams(1),
        name=name,
    )(og, osb, x, wg, ws, g, b)


def _ffn_kernel(*refs, pages_per_step, steps_per_seq):
    if pages_per_step:
        _, h_ref, wu_ref, wd_ref, g_ref, b_ref, q_ref, bias_ref, nw_ref = refs[:9]
        k_refs = refs[9:9 + pages_per_step]
        v_refs = refs[9 + pages_per_step:9 + 2 * pages_per_step]
        o_ref, os_ref, hb_ref, acc_ref, r_ref = refs[9 + 2 * pages_per_step:]
    else:
        h_ref, wu_ref, wd_ref, g_ref, b_ref, o_ref, hb_ref = refs
    f = pl.program_id(1)
    n_f = pl.num_programs(1)

    @pl.when(f == 0)
    def _():
        hb_ref[...] = h_ref[...].astype(BF16)
        o_ref[...] = jnp.zeros_like(o_ref)

    if pages_per_step:
        s = (pl.program_id(0) * n_f + f) % steps_per_seq

        @pl.when(s == 0)
        def _():
            acc_ref[...] = jnp.zeros_like(acc_ref)
            r_ref[...] = jnp.zeros_like(r_ref)

    if pages_per_step:
        zc = _sb_page_scores(q_ref, bias_ref, k_refs)
    a = jnp.maximum(_dot(hb_ref[...], wu_ref[...]), 0.0)
    if pages_per_step:
        cs = _sb_page_suffix(zc)
    o_ref[...] += _dot((a * a).astype(BF16), wd_ref[...])
    if pages_per_step:
        _sb_page_fold(zc, cs, v_refs, acc_ref, r_ref)

        @pl.when(s == steps_per_seq - 1)
        def _():
            o = acc_ref[...]
            o = o * lax.rsqrt(jnp.mean(o * o, axis=-1, keepdims=True) + RMS_EPS) * nw_ref[...]
            os_ref[...] = o.astype(os_ref.dtype)

    @pl.when(f == n_f - 1)
    def _():
        o_ref[...] = _layernorm(DN_ALPHA * h_ref[...] + o_ref[...], g_ref[...], b_ref[...])


def _ffn(h, wu, wd, g, b, *, tm, name, attn=None):
    m, d = h.shape
    tf = FFN_TILE
    n_m, n_f = m // tm, wu.shape[1] // tf
    in_specs = [
        pl.BlockSpec((tm, d), lambda i, f, *_: (i, 0)),
        pl.BlockSpec((d, tf), lambda i, f, *_: (0, f)),
        pl.BlockSpec((tf, d), lambda i, f, *_: (f, 0)),
        pl.BlockSpec((1, d), lambda i, f, *_: (0, 0)),
        pl.BlockSpec((1, d), lambda i, f, *_: (0, 0)),
    ]
    out_specs = [pl.BlockSpec((tm, d), lambda i, f, *_: (i, 0))]
    out_shape = [jax.ShapeDtypeStruct((m, d), F32)]
    scratch = [pltpu.VMEM((tm, d), BF16)]
    args = [h, wu, wd, g, b]
    if attn is None:
        pages_per_step = steps_per_seq = 0
        prefetch = []
    else:
        q, cache_k, cache_v, page_table, bias_tile, sb_norm_w, pages_per_step = attn
        db, n_pages = page_table.shape
        steps_per_seq = n_pages // pages_per_step
        assert n_m * n_f == db * steps_per_seq, "attention steps must tile the FFN grid exactly"
        seq_of = lambda i, f: (i * n_f + f) // steps_per_seq
        first_page = lambda i, f: n_pages - 1 - ((i * n_f + f) % steps_per_seq) * pages_per_step

        def page_spec(j):
            return pl.BlockSpec((None, None, PAGE_SIZE, N_HEADS, HEAD_DIM),
                                lambda i, f, pt: (0, pt[seq_of(i, f), first_page(i, f) - j], 0, 0, 0))

        in_specs += [
            pl.BlockSpec((None, N_HEADS, HEAD_DIM), lambda i, f, pt: (seq_of(i, f), 0, 0)),
            pl.BlockSpec((N_HEADS, LANES), lambda i, f, pt: (0, 0)),
            pl.BlockSpec((N_HEADS, HEAD_DIM), lambda i, f, pt: (0, 0)),
        ] + [page_spec(j) for j in range(pages_per_step)] * 2
        out_specs.append(pl.BlockSpec((None, N_HEADS, HEAD_DIM), lambda i, f, pt: (seq_of(i, f), 0, 0)))
        out_shape.append(jax.ShapeDtypeStruct((db, N_HEADS, HEAD_DIM), F32))
        scratch += [pltpu.VMEM((N_HEADS, HEAD_DIM), F32), pltpu.VMEM((N_HEADS, LANES), F32)]
        args += [q, bias_tile, sb_norm_w] + [cache_k] * pages_per_step + [cache_v] * pages_per_step
        prefetch = [page_table]
    outs = pl.pallas_call(
        functools.partial(_ffn_kernel, pages_per_step=pages_per_step, steps_per_seq=steps_per_seq),
        grid_spec=pltpu.PrefetchScalarGridSpec(
            num_scalar_prefetch=len(prefetch), grid=(n_m, n_f),
            in_specs=in_specs, out_specs=out_specs, scratch_shapes=scratch),
        out_shape=out_shape,
        compiler_params=_cparams(2),
        name=name,
    )(*prefetch, *args)
    return outs[0] if attn is None else outs


def kernel(x_prompt, x_sample, cache_k, cache_v, state_gdn, state_conv, page_table, w_in, conv_w, a_log,
           dt_bias, gdn_norm_w, sb_norm_w, sb_bias, w_out, ln1_g, ln1_b, w_up, w_down, ln2_g, ln2_b):
    bp, sp, d = x_prompt.shape
    db = x_sample.shape[0]
    assert w_in.shape[0] == 1 and x_sample.shape[1] == 1

    w = w_in[0]
    o_z = CONV_DIM
    o_b = o_z + HEADS_WIDTH
    o_a = o_b + N_HEADS
    o_q = o_a + N_HEADS
    o_k = o_q + HEADS_WIDTH
    o_v = o_k + HEADS_WIDTH
    w_main = w[:, :o_b].astype(BF16)
    w_k = w[:, o_k:o_v].astype(BF16)
    w_v = w[:, o_v:].astype(BF16)
    w_qba = jnp.concatenate(
        [w[:, o_q:o_k], w[:, o_b:o_q], jnp.zeros((d, LANES - 2 * N_HEADS), w.dtype)], axis=1).astype(BF16)
    w_og = w_out[0, :HEADS_WIDTH].astype(BF16)
    w_os = w_out[0, HEADS_WIDTH:].astype(BF16)
    wu = w_up[0].astype(BF16)
    wd = w_down[0].astype(BF16)
    pad_row = lambda v: jnp.pad(v, (N_HEADS, LANES - 2 * N_HEADS)).reshape(1, LANES)
    alog_row = pad_row(a_log[0])
    dtb_row = pad_row(dt_bias[0])
    nw_row = gdn_norm_w[0].reshape(1, HEAD_DIM)
    cw = conv_w[0]
    row = lambda v: v.reshape(1, d)
    bias_tile = jnp.broadcast_to(sb_bias[0][:, None], (N_HEADS, LANES))

    xp = x_prompt.reshape(bp * sp, d)
    p1 = _matmul(xp, w_main, tm=1024, tn=1024, name="proj_main_p")
    sqp, bap = _matmul_split(xp, w_qba, tm=1024, name="proj_qba_p",
                             outs=[(HEADS_WIDTH, BF16, SB_QSCALE), (LANES, F32, None)])
    skp, skp_b = _matmul(xp, w_k, tm=1024, tn=1024, name="proj_k_p", out_dtypes=(F32, BF16))
    svp, svp_b = _matmul(xp, w_v, tm=1024, tn=1024, name="proj_v_p", out_dtypes=(F32, BF16))
    og_p, gdn_p = _gdn_prompt(p1, bap, cw, alog_row, dtb_row, nw_row, batch=bp, seq=sp, blk=256)
    os_p = _sb_prompt(sqp, skp_b, svp_b, sb_bias[0], sb_norm_w[0], batch=bp, seq=sp, tq=512)
    h_p = _out_ln(og_p, os_p, xp, w_og, w_os, row(ln1_g[0]), row(ln1_b[0]), tm=512, name="out_ln_p")
    conv_p = p1.reshape(bp, sp, -1)[:, sp - (CONV_W - 1):, :CONV_DIM]

    xs = x_sample.reshape(db, d)
    s1 = _matmul(xs, w_main, tm=db, tn=1024, name="proj_main_s")
    sqs, bas = _matmul_split(xs, w_qba, tm=db, name="proj_qba_s",
                             outs=[(HEADS_WIDTH, F32, SB_QSCALE), (LANES, F32, None)])
    sks = _matmul(xs, w_k, tm=db, tn=1024, name="proj_k_s")
    svs = _matmul(xs, w_v, tm=db, tn=1024, name="proj_v_s")
    og_s, gdn_s, conv_s = _gdn_sample(s1[:, :CONV_DIM], s1[:, CONV_DIM:], bas,
                                      state_conv[0], state_gdn[0], cw, alog_row, dtb_row, nw_row)
    y_p, os_s = _ffn(h_p, wu, wd, row(ln2_g[0]), row(ln2_b[0]), tm=512, name="ffn_p",
                     attn=(sqs.reshape(db, N_HEADS, HEAD_DIM), cache_k, cache_v, page_table, bias_tile,
                           sb_norm_w[0], 8))
    h_s = _out_ln(og_s.reshape(db, HEADS_WIDTH), os_s.reshape(db, HEADS_WIDTH), xs, w_og, w_os,
                  row(ln1_g[0]), row(ln1_b[0]), tm=db, name="out_ln_s")
    y_s = _ffn(h_s, wu, wd, row(ln2_g[0]), row(ln2_b[0]), tm=db, name="ffn_s")

    n_pg = sp // PAGE_SIZE
    return (
        y_p.reshape(bp, sp, d),
        y_s.reshape(db, 1, d),
        skp.reshape(1, bp, n_pg, PAGE_SIZE, N_HEADS, HEAD_DIM),
        svp.reshape(1, bp, n_pg, PAGE_SIZE, N_HEADS, HEAD_DIM),
        gdn_p[None],
        conv_p[None],
        sks.reshape(1, db, 1, N_HEADS, HEAD_DIM),
        svs.reshape(1, db, 1, N_HEADS, HEAD_DIM),
        gdn_s[None],
        conv_s[None],
    )
```
